```python
import math
import jax, jax.numpy as jnp
from jax import lax
import numpy as np

D_MODEL = 1024
BATCH = 16
SEQ = 2048
DEPTH = 2
DEC_BATCH = 32
DEC_SEQ = 32
PAST_LEN = 1024

CHUNK = 64
SUB = 16
D_MIX = 2 * D_MODEL
SSD_WIDTH = D_MIX // 2
SSD_HEADDIM = 64
SSD_HEADS = SSD_WIDTH // SSD_HEADDIM
SSD_GROUPS = 2
SSD_REP = SSD_HEADS // SSD_GROUPS
SSD_STATE = 128
CONV_W = 4
CONV_CH = SSD_WIDTH + 2 * SSD_GROUPS * SSD_STATE
ML_WIDTH = D_MIX // 4
ML_HEADS = 4
ML_HD = ML_WIDTH // ML_HEADS
HG_WIDTH = D_MIX // 4
HG_HEADS = 4
HG_HD = HG_WIDTH // HG_HEADS
D_FF = -(-8 * D_MODEL // (3 * 256)) * 256
IN_SIZES = (SSD_WIDTH, CONV_CH, SSD_HEADS,
            ML_WIDTH, ML_WIDTH, ML_WIDTH, ML_HEADS, ML_HEADS, ML_WIDTH,
            HG_WIDTH, HG_WIDTH, HG_WIDTH, HG_WIDTH)
IN_COLS = sum(IN_SIZES)
EPS = 1e-6

kernel_name = 'hybrid_ssd_mlstm_hgrn2_stream_step'


def rmsnorm(x, gain):
    xf = x.astype(jnp.float32)
    y = xf * lax.rsqrt(jnp.mean(xf * xf, axis=-1, keepdims=True) + EPS)
    return (y * gain.astype(jnp.float32)).astype(x.dtype)


def split_cols(a, sizes):
    idx, acc = [], 0
    for s in sizes[:-1]:
        acc += s
        idx.append(acc)
    return jnp.split(a, idx, axis=-1)


def to_chunks(a, fill):
    b, t = a.shape[:2]
    tp = -(-t // CHUNK) * CHUNK
    pad = [(0, 0), (0, tp - t)] + [(0, 0)] * (a.ndim - 2)
    a = jnp.pad(a.astype(jnp.float32), pad, constant_values=fill)
    a = a.reshape((b, tp // CHUNK, CHUNK) + a.shape[2:])
    return jnp.moveaxis(a, 1, 0)


def from_chunks(a, t):
    a = jnp.moveaxis(a, 0, 1)
    a = a.reshape((a.shape[0], -1) + a.shape[3:])
    return a[:, :t]


def ssd_chunk(h, inp):
    x, a, bm, cm = inp
    L = x.shape[1]
    acum = jnp.cumsum(a, axis=1)
    causal = jnp.tril(jnp.ones((L, L), dtype=bool))
    seg = acum[:, :, None] - acum[:, None, :]
    decay = jnp.exp(jnp.where(causal[None, :, :, None, None], seg, -jnp.inf))
    cb = jnp.einsum('blgn,bsgn->blsg', cm, bm)
    y = jnp.einsum('blsg,blsgr,bsgrp->blgrp', cb, decay, x)
    y = y + jnp.einsum('blgn,bgrpn,blgr->blgrp', cm, h, jnp.exp(acum))
    dec_end = jnp.exp(acum[:, -1:] - acum)
    h_new = h * jnp.exp(acum[:, -1])[..., None, None] + jnp.einsum('blgn,blgr,blgrp->bgrpn', bm, dec_end, x)
    return h_new, y


def mlstm_chunk(carry, inp):
    c, n, m = carry
    q, k, v, ig, lf = inp
    L = q.shape[1]
    causal = jnp.tril(jnp.ones((L, L), dtype=bool))
    fcum = jnp.cumsum(lf, axis=1)
    dmat = fcum[:, :, None] - fcum[:, None, :] + ig[:, None, :]
    dmat = jnp.where(causal[None, :, :, None], dmat, -jnp.inf)
    inter = fcum + m[:, None]
    m_i = jnp.maximum(jnp.max(dmat, axis=2), inter)
    w_intra = jnp.exp(dmat - m_i[:, :, None])
    w_inter = jnp.exp(inter - m_i)
    s = jnp.einsum('blhk,bshk->blsh', q, k) * w_intra
    num = jnp.einsum('blsh,bshv->blhv', s, v) + w_inter[..., None] * jnp.einsum('blhk,bhkv->blhv', q, c)
    den = jnp.sum(s, axis=2) + w_inter * jnp.einsum('blhk,bhk->blh', q, n)
    h = num / jnp.maximum(jnp.abs(den), jnp.exp(-m_i))[..., None]
    f_end = fcum[:, -1]
    dl = f_end[:, None] - fcum + ig
    m_new = jnp.maximum(f_end + m, jnp.max(dl, axis=1))
    wk = jnp.exp(dl - m_new[:, None])
    sc = jnp.exp(f_end + m - m_new)
    c_new = sc[..., None, None] * c + jnp.einsum('blh,blhk,blhv->bhkv', wk, k, v)
    n_new = sc[..., None] * n + jnp.einsum('blh,blhk->bhk', wk, k)
    return (c_new, n_new, m_new), h


def gla_chunk(state, inp):
    q, k, v, g = inp
    b, L, H, K = q.shape
    V = v.shape[-1]
    ns = L // SUB
    gc = jnp.cumsum(g, axis=1)
    gs = gc.reshape(b, ns, SUB, H, K)
    qs = q.reshape(b, ns, SUB, H, K)
    ks = k.reshape(b, ns, SUB, H, K)
    vs = v.reshape(b, ns, SUB, H, V)
    gref = jnp.concatenate([jnp.zeros_like(gs[:, :1, 0]), gs[:, :-1, -1]], axis=1)
    q_ref = qs * jnp.exp(gs - gref[:, :, None])
    before = jnp.arange(L)[None, :] < (jnp.arange(ns) * SUB)[:, None]
    k_ref = k[:, None] * jnp.exp(jnp.where(before[None, :, :, None, None], gref[:, :, None] - gc[:, None], -jnp.inf))
    a_off = jnp.einsum('bashk,bajhk->bhasj', q_ref, k_ref)
    o = jnp.einsum('bhasj,bjhv->bashv', a_off, v)
    tri = jnp.tril(jnp.ones((SUB, SUB), dtype=bool))
    pair = jnp.exp(jnp.where(tri[None, None, :, :, None, None], gs[:, :, :, None] - gs[:, :, None], -jnp.inf))
    a_diag = jnp.einsum('bashk,basthk,bathk->bhast', qs, pair, ks)
    o = o + jnp.einsum('bhast,bathv->bashv', a_diag, vs)
    o = o.reshape(b, L, H, V) + jnp.einsum('blhk,bhkv->blhv', q * jnp.exp(gc), state)
    s_new = jnp.exp(gc[:, -1])[..., None] * state + jnp.einsum('blhk,blhv->bhkv', k * jnp.exp(gc[:, -1:] - gc), v)
    return s_new, o


def trunk_layer(x, state, p):
    conv_st, ssd_st, mc_st, mn_st, mm_st, hg_st = state
    f32 = jnp.float32
    b, t, _ = x.shape
    h = rmsnorm(x, p['norm_mix'])
    proj = jnp.einsum('btd,dc->btc', h, p['w_in'])
    (z, xbc, dt_raw, mq, mk, mv, mi, mf, mo, hq, hf, hi, hgate) = split_cols(proj, IN_SIZES)

    xpad = jnp.concatenate([conv_st.astype(xbc.dtype), xbc], axis=1)
    conv = p['conv_b'] + sum(xpad[:, w:w + t] * p['conv_w'][w] for w in range(CONV_W))
    new_conv = xpad[:, -(CONV_W - 1):]
    xbc = jax.nn.silu(conv)
    xs, bmat, cmat = split_cols(xbc, (SSD_WIDTH, SSD_GROUPS * SSD_STATE, SSD_GROUPS * SSD_STATE))
    dt = jax.nn.softplus(dt_raw.astype(f32) + p['dt_bias'].astype(f32)).reshape(b, t, SSD_GROUPS, SSD_REP)
    a_neg = -jnp.exp(p['a_log'].astype(f32)).reshape(SSD_GROUPS, SSD_REP)
    xh = xs.astype(f32).reshape(b, t, SSD_GROUPS, SSD_REP, SSD_HEADDIM)
    ssd_in = (to_chunks(xh * dt[..., None], 0.0), to_chunks(dt * a_neg, 0.0),
              to_chunks(bmat.reshape(b, t, SSD_GROUPS, SSD_STATE), 0.0),
              to_chunks(cmat.reshape(b, t, SSD_GROUPS, SSD_STATE), 0.0))
    h0 = ssd_st.astype(f32).reshape(b, SSD_GROUPS, SSD_REP, SSD_HEADDIM, SSD_STATE)
    h_fin, y = lax.scan(ssd_chunk, h0, ssd_in)
    y = from_chunks(y, t) + p['d_skip'].astype(f32).reshape(SSD_GROUPS, SSD_REP)[..., None] * xh
    y_ssd = rmsnorm(y.reshape(b, t, SSD_WIDTH) * jax.nn.silu(z.astype(f32)), p['ssd_gain'])

    q = mq.reshape(b, t, ML_HEADS, ML_HD)
    k = mk.reshape(b, t, ML_HEADS, ML_HD) * (ML_HD ** -0.5)
    v = mv.reshape(b, t, ML_HEADS, ML_HD)
    ig = mi.astype(f32) + p['ml_bi'].astype(f32)
    lf = jax.nn.log_sigmoid(mf.astype(f32) + p['ml_bf'].astype(f32))
    carry0 = (mc_st.astype(f32), mn_st.astype(f32), mm_st.astype(f32))
    ml_in = (to_chunks(q, 0.0), to_chunks(k, 0.0), to_chunks(v, 0.0), to_chunks(ig, -jnp.inf), to_chunks(lf, 0.0))
    (mc, mn, mm), hc = lax.scan(mlstm_chunk, carry0, ml_in)
    hc = rmsnorm(from_chunks(hc, t), p['ml_gain'].reshape(ML_HEADS, ML_HD)).reshape(b, t, ML_WIDTH)
    y_ml = jax.nn.sigmoid(mo.astype(f32)) * hc

    fgate = p['lb'] + (1.0 - p['lb']) * jax.nn.sigmoid(hf.astype(f32))
    hq_ = jax.nn.silu(hq.astype(f32)).reshape(b, t, HG_HEADS, HG_HD)
    hk = (1.0 - fgate).reshape(b, t, HG_HEADS, HG_HD)
    hlog = jnp.log(fgate).reshape(b, t, HG_HEADS, HG_HD)
    hv = hi.reshape(b, t, HG_HEADS, HG_HD)
    hg_in = (to_chunks(hq_, 0.0), to_chunks(hk, 0.0), to_chunks(hv, 0.0), to_chunks(hlog, 0.0))
    s_fin, o = lax.scan(gla_chunk, hg_st.astype(f32), hg_in)
    o = rmsnorm(from_chunks(o, t), p['hg_gain'].reshape(HG_HEADS, HG_HD)).reshape(b, t, HG_WIDTH)
    y_hg = o * jax.nn.silu(hgate.astype(f32))

    mix = jnp.concatenate([y_ssd, y_ml, y_hg], axis=-1).astype(x.dtype)
    x = x + jnp.einsum('btc,cd->btd', mix, p['w_out'])
    h2 = rmsnorm(x, p['norm_ffn'])
    gu = jnp.einsum('btd,df->btf', h2, p['w_ffn_in'])
    g, u = jnp.split(gu, 2, axis=-1)
    x = x + jnp.einsum('btf,fd->btd', jax.nn.silu(g) * u, p['w_ffn_out'])
    new_state = (new_conv, h_fin.reshape(b, SSD_HEADS, SSD_HEADDIM, SSD_STATE), mc, mn, mm, s_fin)
    return x, tuple(s.astype(x.dtype) for s in new_state)


def run_trunk(x, states, params, norm_final):
    new = [[] for _ in states]
    for l in range(DEPTH):
        p = {name: arr[l] for name, arr in params.items()}
        x, st = trunk_layer(x, tuple(s[l] for s in states), p)
        for lst, s in zip(new, st):
            lst.append(s)
    return rmsnorm(x, norm_final), tuple(jnp.stack(lst) for lst in new)


def setup_inputs(seed: int = 0) -> dict:
    key = jax.random.key(seed)
    ks = jax.random.split(key, 32)
    f32 = jnp.float32

    def nrm(k, shape, scale):
        return jax.random.normal(k, shape, f32) * scale

    def gain(k, shape, s=0.02):
        return 1.0 + s * jax.random.normal(k, shape, f32)

    dt0 = jnp.exp(jax.random.uniform(ks[12], (DEPTH, SSD_HEADS), f32, math.log(1e-3), math.log(1e-1)))
    return {
        'x_prompt': nrm(ks[0], (BATCH, SEQ, D_MODEL), 1.0),
        'x_sample': nrm(ks[1], (DEC_BATCH, DEC_SEQ, D_MODEL), 1.0),
        'state_conv': nrm(ks[2], (DEPTH, DEC_BATCH, CONV_W - 1, CONV_CH), 1.0),
        'state_ssd': nrm(ks[3], (DEPTH, DEC_BATCH, SSD_HEADS, SSD_HEADDIM, SSD_STATE), 0.1),
        'state_mlstm_c': nrm(ks[4], (DEPTH, DEC_BATCH, ML_HEADS, ML_HD, ML_HD), 0.1),
        'state_mlstm_n': nrm(ks[5], (DEPTH, DEC_BATCH, ML_HEADS, ML_HD), 0.1),
        'state_mlstm_m': nrm(ks[6], (DEPTH, DEC_BATCH, ML_HEADS), 1.0),
        'state_hgrn': nrm(ks[7], (DEPTH, DEC_BATCH, HG_HEADS, HG_HD, HG_HD), 0.5),
        'norm_mix': gain(ks[8], (DEPTH, D_MODEL)),
        'w_in': nrm(ks[9], (DEPTH, D_MODEL, IN_COLS), D_MODEL ** -0.5),
        'conv_w': nrm(ks[10], (DEPTH, CONV_W, CONV_CH), CONV_W ** -0.5),
        'conv_b': nrm(ks[11], (DEPTH, CONV_CH), 0.01),
        'dt_bias': dt0 + jnp.log(-jnp.expm1(-dt0)),
        'a_log': jnp.log(jax.random.uniform(ks[13], (DEPTH, SSD_HEADS), f32, 1.0, 16.0)),
        'd_skip': gain(ks[14], (DEPTH, SSD_HEADS), 0.1),
        'ssd_gain': gain(ks[15], (DEPTH, SSD_WIDTH)),
        'ml_bi': nrm(ks[16], (DEPTH, ML_HEADS), 0.1),
        'ml_bf': jnp.linspace(3.0, 6.0, ML_HEADS, dtype=f32)[None] + nrm(ks[17], (DEPTH, ML_HEADS), 0.1),
        'ml_gain': gain(ks[18], (DEPTH, ML_WIDTH)),
        'hg_lb': nrm(ks[19], (DEPTH, HG_WIDTH), 0.5),
        'hg_gain': gain(ks[20], (DEPTH, HG_WIDTH)),
        'w_out': nrm(ks[21], (DEPTH, D_MIX, D_MODEL), D_MIX ** -0.5),
        'norm_ffn': gain(ks[22], (DEPTH, D_MODEL)),
        'w_ffn_in': nrm(ks[23], (DEPTH, D_MODEL, 2 * D_FF), D_MODEL ** -0.5),
        'w_ffn_out': nrm(ks[24], (DEPTH, D_FF, D_MODEL), D_FF ** -0.5),
        'norm_final': gain(ks[25], (D_MODEL,)),
    }


def reference(x_prompt, x_sample, state_conv, state_ssd, state_mlstm_c, state_mlstm_n, state_mlstm_m, state_hgrn,
              norm_mix, w_in, conv_w, conv_b, dt_bias, a_log, d_skip, ssd_gain, ml_bi, ml_bf, ml_gain,
              hg_lb, hg_gain, w_out, norm_ffn, w_ffn_in, w_ffn_out, norm_final):
    lb_soft = jax.nn.softmax(hg_lb.astype(jnp.float32), axis=0)
    hg_lower = jnp.cumsum(lb_soft, axis=0) - lb_soft[0]
    params = {'norm_mix': norm_mix, 'w_in': w_in, 'conv_w': conv_w, 'conv_b': conv_b, 'dt_bias': dt_bias,
              'a_log': a_log, 'd_skip': d_skip, 'ssd_gain': ssd_gain, 'ml_bi': ml_bi, 'ml_bf': ml_bf,
              'ml_gain': ml_gain, 'lb': hg_lower, 'hg_gain': hg_gain, 'w_out': w_out, 'norm_ffn': norm_ffn,
              'w_ffn_in': w_ffn_in, 'w_ffn_out': w_ffn_out}
    bp = x_prompt.shape[0]
    f32 = jnp.float32
    zero_state = (jnp.zeros((DEPTH, bp, CONV_W - 1, CONV_CH), x_prompt.dtype),
                  jnp.zeros((DEPTH, bp, SSD_HEADS, SSD_HEADDIM, SSD_STATE), f32),
                  jnp.zeros((DEPTH, bp, ML_HEADS, ML_HD, ML_HD), f32),
                  jnp.zeros((DEPTH, bp, ML_HEADS, ML_HD), f32),
                  jnp.zeros((DEPTH, bp, ML_HEADS), f32),
                  jnp.zeros((DEPTH, bp, HG_HEADS, HG_HD, HG_HD), f32))
    y_prompt, (p_conv, p_ssd, p_mc, p_mn, p_mm, p_hg) = run_trunk(x_prompt, zero_state, params, norm_final)
    sample_state = (state_conv, state_ssd, state_mlstm_c, state_mlstm_n, state_mlstm_m, state_hgrn)
    y_sample, (s_conv, s_ssd, s_mc, s_mn, s_mm, s_hg) = run_trunk(x_sample, sample_state, params, norm_final)
    return (y_prompt, y_sample, p_conv, p_ssd, p_mc, p_mn, p_mm, p_hg, s_conv, s_ssd, s_mc, s_mn, s_mm, s_hg)
```

```python
import functools

import numpy as np
import jax
import jax.numpy as jnp
from jax import lax
from jax.experimental import pallas as pl
from jax.experimental.pallas import tpu as pltpu

D_MODEL = 1024
CHUNK = 64
SSD_WIDTH = 1024
SSD_HEADDIM = 64
SSD_HEADS = 16
SSD_GROUPS = 2
SSD_STATE = 128
CONV_W = 4
CONV_CH = SSD_WIDTH + 2 * SSD_GROUPS * SSD_STATE
ML_WIDTH = 512
ML_HEADS = 4
ML_HD = 128
HG_WIDTH = 512
HG_HEADS = 4
HG_HD = 128
D_MIX = 2048
D_FF = 2816
EPS = 1e-6

C_Z = 0
C_XBC = 1024
C_MQ, C_MK, C_MV, C_MO = 2560, 3072, 3584, 4096
C_HQ, C_HF, C_HI, C_HG = 4608, 5120, 5632, 6144
C_SM = 6656
PROJ_COLS = 6912
PROJ_TILE = 1152
LANE_DT = 0
LANE_MI = 16
LANE_MF = 20

VMEM_LIMIT = 52 * 1024 * 1024
NEG_INF = float("-inf")

HG_LEVELS = (32, 16, 8, 4, 2, 1)
N_EXP_BLOCKS = len(HG_LEVELS) + 2

_NT = (((1,), (1,)), ((), ()))


def _bf(x):
    return x.astype(jnp.bfloat16)


def _dot(a, b):
    return jnp.dot(a, b, preferred_element_type=jnp.float32)


def _dot_nt(a, b):
    return lax.dot_general(a, b, _NT, preferred_element_type=jnp.float32)


def _silu(x):
    return x * (1.0 / (1.0 + jnp.exp(-x)))


def _sigmoid(x):
    return 1.0 / (1.0 + jnp.exp(-x))


def _softplus(x):
    return jnp.maximum(x, 0.0) + jnp.log(1.0 + jnp.exp(-jnp.abs(x)))


def _split3(x):
    hi = _bf(x)
    r1 = x - hi.astype(jnp.float32)
    mid = _bf(r1)
    r2 = r1 - mid.astype(jnp.float32)
    return hi, mid, _bf(r2)


def _rms(x, gain):
    return x * lax.rsqrt(jnp.mean(x * x, axis=-1, keepdims=True) + EPS) * gain


def _in_proj_kernel(x_ref, g_ref, w_ref, o_ref):
    h = _bf(_rms(x_ref[...], g_ref[...]))
    for j in range(PROJ_COLS // PROJ_TILE):
        sl = slice(j * PROJ_TILE, (j + 1) * PROJ_TILE)
        o_ref[:, sl] = _dot(h, w_ref[:, sl])


def _in_proj(x2d, gain, w, tm):
    n = x2d.shape[0]
    return pl.pallas_call(
        _in_proj_kernel,
        grid=(n // tm,),
        in_specs=[
            pl.BlockSpec((tm, D_MODEL), lambda i: (i, 0)),
            pl.BlockSpec((1, D_MODEL), lambda i: (0, 0)),
            pl.BlockSpec((D_MODEL, PROJ_COLS), lambda i: (0, 0), pipeline_mode=pl.Buffered(1)),
        ],
        out_specs=pl.BlockSpec((tm, PROJ_COLS), lambda i: (i, 0)),
        out_shape=jax.ShapeDtypeStruct((n, PROJ_COLS), jnp.float32),
        compiler_params=pltpu.CompilerParams(
            dimension_semantics=("parallel",), vmem_limit_bytes=VMEM_LIMIT),
        name="in_proj",
    )(x2d, gain, w)


FF_TILE = 256


def _out_ffn_kernel(x_ref, mix_ref, wo_ref, gn_ref, wi_ref, wf_ref, gf_ref, o_ref, *, final_norm):
    x1 = x_ref[...] + _dot(mix_ref[...], wo_ref[...])
    h2 = _bf(_rms(x1, gn_ref[...]))
    acc = x1
    for j in range(D_FF // FF_TILE):
        g = _dot(h2, wi_ref[:, j * FF_TILE:(j + 1) * FF_TILE])
        u = _dot(h2, wi_ref[:, D_FF + j * FF_TILE:D_FF + (j + 1) * FF_TILE])
        acc = acc + _dot(_bf(_silu(g) * u), wf_ref[j * FF_TILE:(j + 1) * FF_TILE, :])
    if final_norm:
        acc = _rms(acc, gf_ref[...])
    o_ref[...] = acc


def _out_ffn(x2d, mix2d, w_out, g_ffn, w_fi, w_fo, g_final, tm, final_norm):
    n = x2d.shape[0]
    const = lambda i: (0, 0)
    one = pl.Buffered(1)
    return pl.pallas_call(
        functools.partial(_out_ffn_kernel, final_norm=final_norm),
        grid=(n // tm,),
        in_specs=[
            pl.BlockSpec((tm, D_MODEL), lambda i: (i, 0)),
            pl.BlockSpec((tm, D_MIX), lambda i: (i, 0)),
            pl.BlockSpec((D_MIX, D_MODEL), const, pipeline_mode=one),
            pl.BlockSpec((1, D_MODEL), const),
            pl.BlockSpec((D_MODEL, 2 * D_FF), const, pipeline_mode=one),
            pl.BlockSpec((D_FF, D_MODEL), const, pipeline_mode=one),
            pl.BlockSpec((1, D_MODEL), const),
        ],
        out_specs=pl.BlockSpec((tm, D_MODEL), lambda i: (i, 0)),
        out_shape=jax.ShapeDtypeStruct((n, D_MODEL), jnp.float32),
        compiler_params=pltpu.CompilerParams(
            dimension_semantics=("parallel",), vmem_limit_bytes=VMEM_LIMIT),
        name="out_ffn",
    )(x2d, mix2d, w_out, g_ffn, w_fi, w_fo, g_final)


def _mixer_constants():
    L = CHUNK
    e = np.zeros((128, SSD_WIDTH), np.float32)
    for h in range(SSD_HEADS):
        e[h, h * SSD_HEADDIM:(h + 1) * SSD_HEADDIM] = 1.0
    e3 = np.concatenate([e, e, e], axis=0)

    l = np.arange(L)[:, None]
    c = np.arange(SSD_WIDTH)[None, :]
    dsel = ((c % L) == l).astype(np.float32)
    s2 = np.arange(128)[None, :] % L
    causal2 = (s2 <= l).astype(np.float32)

    t = np.arange(L)
    blocks = []
    masks = []
    isq = np.zeros((L, 128), np.float32)
    for li, m in enumerate(HG_LEVELS):
        cm = np.zeros((L, L), np.float32)
        start = (t // (2 * m)) * (2 * m)
        mid = start + m - 1
        query = (t - start) >= m
        for s in range(L):
            if query[s]:
                cm[s, mid[s] + 1:s + 1] = 1.0
            else:
                cm[s, s + 1:mid[s] + 1] = 1.0
        blocks.append(cm)
        same = (start[:, None] == start[None, :])
        mk = same & query[:, None] & (~query[None, :])
        masks.append(np.concatenate([mk, mk], axis=1).astype(np.float32))
        isq[:, li] = query.astype(np.float32)
    tri = (t[None, :] <= t[:, None]).astype(np.float32)
    blocks.append(tri)
    blocks.append((t[None, :] > t[:, None]).astype(np.float32))
    ce = np.concatenate(blocks, axis=0)
    ce3 = np.concatenate([ce, ce, ce], axis=1)
    diag = (t[:, None] == t[None, :])
    masks.append(np.concatenate([diag, diag], axis=1).astype(np.float32))
    hmask = np.stack(masks, axis=0)
    return (jnp.asarray(e3, jnp.bfloat16), jnp.asarray(dsel), jnp.asarray(causal2),
            jnp.asarray(ce3, jnp.bfloat16), jnp.asarray(hmask), jnp.asarray(isq))


def _scan_time(x, op, identity, row):
    sh = 1
    while sh < x.shape[0]:
        r = pltpu.roll(x, sh, axis=0)
        x = op(x, jnp.where(row >= sh, r, identity))
        sh *= 2
    return x


def _blockdiag_rows(a, b):
    z = jnp.zeros_like(a)
    return jnp.concatenate([jnp.concatenate([a, z], axis=1), jnp.concatenate([z, b], axis=1)], axis=0)


def _mixer_kernel(proj_ref, conv0_ref, ssd0_ref, mc0_ref, mn0_ref, mm0_ref, hg0_ref,
                  sp_ref, cw_ref, cb_ref, dsk_ref, sg_ref, mlg_ref, hlb_ref, hgg_ref,
                  e3_ref, dsel_ref, causal_ref, ce3_ref, hmask_ref, isq_ref,
                  mix_ref, conv_ref, ssd_ref, mc_ref, mn_ref, mm_ref, hg_ref,
                  cbuf_ref, *, t_valid, layer):
    L = CHUNK
    f32 = jnp.float32
    c = pl.program_id(1)
    last = pl.num_programs(1) - 1

    @pl.when(c == 0)
    def _init():
        cbuf_ref[5:8, :] = conv0_ref[0]
        ssd_ref[...] = ssd0_ref[...]
        mc_ref[...] = mc0_ref[...]
        mn_ref[...] = mn0_ref[...]
        mm_ref[...] = mm0_ref[...]
        hg_ref[...] = hg0_ref[...]

    row = lax.broadcasted_iota(jnp.int32, (L, 128), 0)
    lane = lax.broadcasted_iota(jnp.int32, (L, 128), 1)
    lo_half = lane < 64
    causal2 = causal_ref[...] > 0.5

    sm = proj_ref[0, :, C_SM:C_SM + 128]
    dt = _softplus(sm + sp_ref[0:1, :])
    a_neg = jnp.where(lane[0:1] < SSD_HEADS, -jnp.exp(sp_ref[1:2, :]), 0.0)
    ig = pltpu.roll(sm, LANE_MF - LANE_MI, axis=1) + sp_ref[2:3, :]
    lf = -_softplus(-(sm + sp_ref[3:4, :]))
    if t_valid < L:
        ok = row < t_valid
        dt = jnp.where(ok, dt, 0.0)
        ig = jnp.where(ok, ig, NEG_INF)
        lf = jnp.where(ok, lf, 0.0)
    ml_lane = (lane >= LANE_MF) & (lane < LANE_MF + ML_HEADS)
    a = dt * a_neg
    cs = _scan_time(jnp.where(lane < SSD_HEADS, a, jnp.where(ml_lane, lf, 0.0)), jnp.add, 0.0, row)
    acum = cs
    u = jnp.where(ml_lane, ig - cs, 0.0)
    m_prev = mm_ref[0]
    m_run = jnp.maximum(_scan_time(u, jnp.maximum, NEG_INF, row), m_prev)
    m_i = cs + m_run
    m_last = m_run[L - 1:L, :]
    w_inter = jnp.exp(m_prev - m_run)
    inv_floor = jnp.exp(-m_i)
    wk_s = jnp.exp(u - m_last)
    sc_s = jnp.exp(m_prev - m_last)

    acum_end = acum[L - 1:L, :]
    stack = jnp.concatenate([dt, acum, jnp.exp(acum), jnp.exp(acum_end - acum)], axis=0)
    hi, mid, lo = _split3(stack)
    ex = _dot(jnp.concatenate([hi, mid, lo], axis=1), e3_ref[...])
    dt_f, ac_f, eac_f, dend_f = ex[0:L], ex[L:2 * L], ex[2 * L:3 * L], ex[3 * L:4 * L]

    xbc = proj_ref[0, :, C_XBC:C_XBC + CONV_CH]
    cbuf_ref[8:8 + L, :] = xbc
    conv = cb_ref[...] + xbc * cw_ref[3:4, :]
    for w in range(CONV_W - 1):
        conv = conv + cbuf_ref[5 + w:5 + w + L, :] * cw_ref[w:w + 1, :]
    cbuf_ref[5:8, :] = xbc[t_valid - 3:t_valid, :]
    xbc_a = _silu(conv)
    xs = xbc_a[:, 0:SSD_WIDTH]
    xt = xs * dt_f
    ctr = jnp.sum(jnp.where(dsel_ref[...] > 0.5, ac_f, 0.0), axis=0, keepdims=True)
    xdec = _bf(xt * dend_f)
    eac_end = eac_f[L - 1:L, :]

    y_parts = []
    for g in range(SSD_GROUPS):
        bm = xbc_a[:, SSD_WIDTH + g * SSD_STATE:SSD_WIDTH + (g + 1) * SSD_STATE]
        cm = xbc_a[:, SSD_WIDTH + (SSD_GROUPS + g) * SSD_STATE:SSD_WIDTH + (SSD_GROUPS + g + 1) * SSD_STATE]
        bm_b, cm_b = _bf(bm), _bf(cm)
        cb2 = _dot_nt(cm_b, jnp.concatenate([bm_b, bm_b], axis=0))
        gs = slice(g * 512, (g + 1) * 512)
        y_inter = _dot(cm_b, _bf(ssd_ref[0, :, gs]))
        for pp in range(4):
            p = g * 4 + pp
            ps = slice(p * 128, (p + 1) * 128)
            seg = ac_f[:, ps] - ctr[:, ps]
            m2 = _bf(cb2 * jnp.exp(jnp.where(causal2, seg, NEG_INF)))
            xp = xt[:, ps]
            xb = _bf(jnp.concatenate([jnp.where(lo_half, xp, 0.0), jnp.where(lo_half, 0.0, xp)], axis=0))
            y_parts.append(_dot(m2, xb) + eac_f[:, ps] * y_inter[:, pp * 128:(pp + 1) * 128])
        ssd_ref[0, :, gs] = ssd_ref[0, :, gs] * eac_end[:, gs] + _dot(_bf(bm.T), xdec[:, gs])
    y = jnp.concatenate(y_parts, axis=1) + dsk_ref[...] * xs
    y_ssd = _rms(y * _silu(proj_ref[0, :, C_Z:C_Z + SSD_WIDTH]), sg_ref[...])
    mix_ref[0, :, 0:SSD_WIDTH] = _bf(y_ssd)

    dsel128 = dsel_ref[:, 0:128] > 0.5
    for pp in range(ML_HEADS // 2):
        h0, h1 = 2 * pp, 2 * pp + 1
        l0, l1 = LANE_MF + h0, LANE_MF + h1
        m_row = jnp.where(lo_half, m_run[:, l0:l0 + 1], m_run[:, l1:l1 + 1])
        u_row = jnp.where(lo_half, u[:, l0:l0 + 1], u[:, l1:l1 + 1])
        u_col = jnp.sum(jnp.where(dsel128, u_row, 0.0), axis=0, keepdims=True)
        wgt = jnp.exp(jnp.where(causal2, u_col - m_row, NEG_INF))
        q2 = _bf(proj_ref[0, :, C_MQ + pp * 256:C_MQ + (pp + 1) * 256])
        k0 = proj_ref[0, :, C_MK + h0 * 128:C_MK + (h0 + 1) * 128] * (ML_HD ** -0.5)
        k1 = proj_ref[0, :, C_MK + h1 * 128:C_MK + (h1 + 1) * 128] * (ML_HD ** -0.5)
        v0 = proj_ref[0, :, C_MV + h0 * 128:C_MV + (h0 + 1) * 128]
        v1 = proj_ref[0, :, C_MV + h1 * 128:C_MV + (h1 + 1) * 128]
        s2 = _dot_nt(q2, _bf(_blockdiag_rows(k0, k1))) * wgt
        num2 = _dot(_bf(s2), _bf(_blockdiag_rows(v0, v1)))
        dens = (jnp.sum(jnp.where(lo_half, s2, 0.0), axis=-1, keepdims=True),
                jnp.sum(jnp.where(lo_half, 0.0, s2), axis=-1, keepdims=True))
        for j, (h, kh, vh) in enumerate(((h0, k0, v0), (h1, k1, v1))):
            ln = LANE_MF + h
            qh = proj_ref[0, :, C_MQ + h * 128:C_MQ + (h + 1) * 128]
            wi = w_inter[:, ln:ln + 1]
            num = num2[:, j * 128:(j + 1) * 128] + wi * _dot(_bf(qh), _bf(mc_ref[0, h]))
            qn = jnp.sum(qh * mn_ref[0, h:h + 1, :], axis=-1, keepdims=True)
            den = dens[j] + wi * qn
            hval = num * (1.0 / jnp.maximum(jnp.abs(den), inv_floor[:, ln:ln + 1]))
            hn = _rms(hval, mlg_ref[:, h * 128:(h + 1) * 128])
            mo = proj_ref[0, :, C_MO + h * 128:C_MO + (h + 1) * 128]
            mix_ref[0, :, SSD_WIDTH + h * 128:SSD_WIDTH + (h + 1) * 128] = _bf(_sigmoid(mo) * hn)
            kw = kh * wk_s[:, ln:ln + 1]
            sc = sc_s[:, ln:ln + 1]
            mc_ref[0, h] = sc * mc_ref[0, h] + _dot(_bf(kw.T), _bf(vh))
            mn_ref[0, h:h + 1, :] = sc * mn_ref[0, h:h + 1, :] + jnp.sum(kw, axis=0, keepdims=True)
    mm_ref[0] = jnp.where(ml_lane[0:1], m_i[L - 1:L, :], 0.0)

    lb_all = hlb_ref[...]
    lb_max = jnp.max(lb_all, axis=0, keepdims=True)
    lb_e = jnp.exp(lb_all - lb_max)
    lb_soft = lb_e * (1.0 / jnp.sum(lb_e, axis=0, keepdims=True))
    lb = jnp.sum(lb_soft[0:layer + 1], axis=0, keepdims=True) - lb_soft[0:1]

    fg = lb + (1.0 - lb) * _sigmoid(proj_ref[0, :, C_HF:C_HF + HG_WIDTH])
    kk = 1.0 - fg
    glog = jnp.log(fg)
    if t_valid < L:
        glog = jnp.where(lax.broadcasted_iota(jnp.int32, (L, HG_WIDTH), 0) < t_valid, glog, 0.0)
    qq = _silu(proj_ref[0, :, C_HQ:C_HQ + HG_WIDTH])
    vv = proj_ref[0, :, C_HI:C_HI + HG_WIDTH]
    ghi, gmid, glo = _split3(glog)
    exps = _dot(ce3_ref[...], jnp.concatenate([ghi, gmid, glo], axis=0))
    nl = len(HG_LEVELS)
    qk_lv = []
    for li in range(nl):
        isq = isq_ref[:, li:li + 1] > 0.5
        qk_lv.append(_bf(jnp.where(isq, qq, kk) * jnp.exp(exps[li * L:(li + 1) * L])))
    gc = exps[nl * L:(nl + 1) * L]
    gend = exps[(nl + 1) * L:(nl + 2) * L]
    q_dec = _bf(qq * jnp.exp(gc))
    k_dec = _bf(kk * jnp.exp(gend))
    s_dec = jnp.exp(gc[L - 1:L, :])
    qq_b, kk_b, vv_b = _bf(qq), _bf(kk), _bf(vv)

    for pp in range(HG_HEADS // 2):
        h0, h1 = 2 * pp, 2 * pp + 1
        s0, s1 = slice(h0 * 128, (h0 + 1) * 128), slice(h1 * 128, (h1 + 1) * 128)
        pr = slice(pp * 256, (pp + 1) * 256)
        a2 = jnp.where(hmask_ref[nl] > 0.5, _dot_nt(qq_b[:, pr], _blockdiag_rows(kk_b[:, s0], kk_b[:, s1])), 0.0)
        for li in range(nl):
            x = qk_lv[li]
            a_l = _dot_nt(x[:, pr], _blockdiag_rows(x[:, s0], x[:, s1]))
            a2 = jnp.where(hmask_ref[li] > 0.5, a_l, a2)
        o2 = _dot(_bf(a2), _blockdiag_rows(vv_b[:, s0], vv_b[:, s1]))
        for j, (h, hs) in enumerate(((h0, s0), (h1, s1))):
            st = hg_ref[0, h]
            o = o2[:, j * 128:(j + 1) * 128] + _dot_nt(q_dec[:, hs], _bf(st))
            on = _rms(o, hgg_ref[:, hs])
            gate = _silu(proj_ref[0, :, C_HG + h * 128:C_HG + (h + 1) * 128])
            mix_ref[0, :, SSD_WIDTH + ML_WIDTH + h * 128:SSD_WIDTH + ML_WIDTH + (h + 1) * 128] = _bf(on * gate)
            hg_ref[0, h] = st * s_dec[:, hs] + _dot(_bf(vv[:, hs].T), k_dec[:, hs])

    @pl.when(c == last)
    def _fin():
        conv_ref[0] = xbc[t_valid - 3:t_valid, :]


def _mixer(proj, conv0, ssd0, mc0, mn0, mm0, hg0, sp, cw, cb, dsk, sg, mlg, hlb, hgg, consts, t_valid, layer):
    b, t, _ = proj.shape
    nc = t // CHUNK
    e3, dsel, causal2, ce3, hmask, isq = consts
    st3 = lambda i, j: (i, 0, 0)
    st4 = lambda i, j: (i, 0, 0, 0)
    c2 = lambda i, j: (0, 0)
    c3 = lambda i, j: (0, 0, 0)
    in_specs = [
        pl.BlockSpec((1, CHUNK, PROJ_COLS), lambda i, j: (i, j, 0)),
        pl.BlockSpec((1, CONV_W - 1, CONV_CH), st3),
        pl.BlockSpec((1, SSD_STATE, SSD_WIDTH), st3),
        pl.BlockSpec((1, ML_HEADS, ML_HD, ML_HD), st4),
        pl.BlockSpec((1, ML_HEADS, ML_HD), st3),
        pl.BlockSpec((1, 1, 128), st3),
        pl.BlockSpec((1, HG_HEADS, HG_HD, HG_HD), st4),
        pl.BlockSpec(sp.shape, c2), pl.BlockSpec(cw.shape, c2), pl.BlockSpec(cb.shape, c2),
        pl.BlockSpec(dsk.shape, c2), pl.BlockSpec(sg.shape, c2), pl.BlockSpec(mlg.shape, c2),
        pl.BlockSpec(hlb.shape, c2), pl.BlockSpec(hgg.shape, c2),
        pl.BlockSpec(e3.shape, c2), pl.BlockSpec(dsel.shape, c2), pl.BlockSpec(causal2.shape, c2),
        pl.BlockSpec(ce3.shape, c2), pl.BlockSpec(hmask.shape, c3), pl.BlockSpec(isq.shape, c2),
    ]
    out_specs = [
        pl.BlockSpec((1, CHUNK, D_MIX), lambda i, j: (i, j, 0)),
        pl.BlockSpec((1, CONV_W - 1, CONV_CH), st3),
        pl.BlockSpec((1, SSD_STATE, SSD_WIDTH), st3),
        pl.BlockSpec((1, ML_HEADS, ML_HD, ML_HD), st4),
        pl.BlockSpec((1, ML_HEADS, ML_HD), st3),
        pl.BlockSpec((1, 1, 128), st3),
        pl.BlockSpec((1, HG_HEADS, HG_HD, HG_HD), st4),
    ]
    out_shape = [
        jax.ShapeDtypeStruct((b, t, D_MIX), jnp.bfloat16),
        jax.ShapeDtypeStruct((b, CONV_W - 1, CONV_CH), jnp.float32),
        jax.ShapeDtypeStruct((b, SSD_STATE, SSD_WIDTH), jnp.float32),
        jax.ShapeDtypeStruct((b, ML_HEADS, ML_HD, ML_HD), jnp.float32),
        jax.ShapeDtypeStruct((b, ML_HEADS, ML_HD), jnp.float32),
        jax.ShapeDtypeStruct((b, 1, 128), jnp.float32),
        jax.ShapeDtypeStruct((b, HG_HEADS, HG_HD, HG_HD), jnp.float32),
    ]
    return pl.pallas_call(
        functools.partial(_mixer_kernel, t_valid=t_valid, layer=layer),
        grid=(b, nc),
        in_specs=in_specs,
        out_specs=out_specs,
        out_shape=out_shape,
        scratch_shapes=[pltpu.VMEM((8 + CHUNK, CONV_CH), jnp.float32)],
        compiler_params=pltpu.CompilerParams(
            dimension_semantics=("parallel", "arbitrary"), vmem_limit_bytes=VMEM_LIMIT),
        name="mixer",
    )(proj, conv0, ssd0, mc0, mn0, mm0, hg0, sp, cw, cb, dsk, sg, mlg, hlb, hgg,
      e3, dsel, causal2, ce3, hmask, isq)


def _regroup_w_in(w):
    o = {}
    acc = 0
    for name, size in (("z", 1024), ("xbc", 1536), ("dt", 16), ("mq", 512), ("mk", 512), ("mv", 512),
                       ("mi", 4), ("mf", 4), ("mo", 512), ("hq", 512), ("hf", 512), ("hi", 512), ("hg", 512)):
        o[name] = w[:, acc:acc + size]
        acc += size
    pad = jnp.zeros((w.shape[0], PROJ_COLS - C_SM - 24), w.dtype)
    cols = [o["z"], o["xbc"], o["mq"], o["mk"], o["mv"], o["mo"], o["hq"], o["hf"], o["hi"], o["hg"],
            o["dt"], o["mi"], o["mf"], pad]
    return jnp.concatenate(cols, axis=1).astype(jnp.bfloat16)


def _small_params(dt_bias, a_log, ml_bi, ml_bf):
    sp = jnp.zeros((8, 128), jnp.float32)
    sp = sp.at[0, LANE_DT:LANE_DT + SSD_HEADS].set(dt_bias)
    sp = sp.at[1, LANE_DT:LANE_DT + SSD_HEADS].set(a_log)
    sp = sp.at[2, LANE_MF:LANE_MF + ML_HEADS].set(ml_bi)
    sp = sp.at[3, LANE_MF:LANE_MF + ML_HEADS].set(ml_bf)
    return sp


def _run_trunk(x, states, lp, norm_final, consts, tm):
    b, t, _ = x.shape
    tp = -(-t // CHUNK) * CHUNK
    assert t % CHUNK == 0 or tp == CHUNK, "a partial chunk is only supported for single-chunk sequences"
    t_valid = t - (tp - CHUNK)
    depth = len(lp)
    x2d = x.reshape(b * t, D_MODEL)
    outs = []
    for l in range(depth):
        p = lp[l]
        proj = _in_proj(x2d, p["norm_mix"], p["w_in"], tm).reshape(b, t, PROJ_COLS)
        if tp != t:
            proj = jnp.pad(proj, ((0, 0), (0, tp - t), (0, 0)))
        res = _mixer(proj, *states[l], p["sp"], p["conv_w"], p["conv_b"], p["d_skip"], p["ssd_gain"],
                     p["ml_gain"], p["hg_lb"], p["hg_gain"], consts, t_valid, l)
        mix = res[0]
        if tp != t:
            mix = mix[:, :t]
        outs.append(res[1:])
        x2d = _out_ffn(x2d, mix.reshape(b * t, D_MIX), p["w_out"], p["norm_ffn"], p["w_ffn_in"],
                       p["w_ffn_out"], norm_final, tm, l == depth - 1)
    return x2d.reshape(b, t, D_MODEL), outs


def _states_to_kernel(conv, ssd, mc, mn, mm, hg):
    b = conv.shape[0]
    ssd_t = jnp.transpose(ssd, (0, 3, 1, 2)).reshape(b, SSD_STATE, SSD_WIDTH)
    mm_p = jnp.zeros((b, 1, 128), jnp.float32).at[:, 0, LANE_MF:LANE_MF + ML_HEADS].set(mm)
    return (conv, ssd_t, mc, mn, mm_p, jnp.swapaxes(hg, -1, -2))


def _states_from_kernel(outs):
    cols = [[] for _ in range(6)]
    for conv, ssd_t, mc, mn, mm_p, hg_t in outs:
        b = conv.shape[0]
        cols[0].append(conv)
        cols[1].append(jnp.transpose(ssd_t.reshape(b, SSD_STATE, SSD_HEADS, SSD_HEADDIM), (0, 2, 3, 1)))
        cols[2].append(mc)
        cols[3].append(mn)
        cols[4].append(mm_p[:, 0, LANE_MF:LANE_MF + ML_HEADS])
        cols[5].append(jnp.swapaxes(hg_t, -1, -2))
    return tuple(jnp.stack(c) for c in cols)


def kernel(x_prompt, x_sample, state_conv, state_ssd, state_mlstm_c, state_mlstm_n, state_mlstm_m, state_hgrn,
           norm_mix, w_in, conv_w, conv_b, dt_bias, a_log, d_skip, ssd_gain, ml_bi, ml_bf, ml_gain,
           hg_lb, hg_gain, w_out, norm_ffn, w_ffn_in, w_ffn_out, norm_final):
    depth = w_in.shape[0]
    f32 = jnp.float32
    consts = _mixer_constants()
    lp = []
    for l in range(depth):
        lp.append({
            "norm_mix": norm_mix[l].reshape(1, D_MODEL),
            "w_in": _regroup_w_in(w_in[l]),
            "sp": _small_params(dt_bias[l], a_log[l], ml_bi[l], ml_bf[l]),
            "conv_w": conv_w[l],
            "conv_b": conv_b[l].reshape(1, CONV_CH),
            "d_skip": jnp.repeat(d_skip[l], SSD_HEADDIM).reshape(1, SSD_WIDTH),
            "ssd_gain": ssd_gain[l].reshape(1, SSD_WIDTH),
            "ml_gain": ml_gain[l].reshape(1, ML_WIDTH),
            "hg_lb": hg_lb.astype(f32),
            "hg_gain": hg_gain[l].reshape(1, HG_WIDTH),
            "w_out": w_out[l].astype(jnp.bfloat16),
            "norm_ffn": norm_ffn[l].reshape(1, D_MODEL),
            "w_ffn_in": w_ffn_in[l].astype(jnp.bfloat16),
            "w_ffn_out": w_ffn_out[l].astype(jnp.bfloat16),
        })
    nf = norm_final.reshape(1, D_MODEL)

    bp = x_prompt.shape[0]
    zero = (jnp.zeros((bp, CONV_W - 1, CONV_CH), f32), jnp.zeros((bp, SSD_STATE, SSD_WIDTH), f32),
            jnp.zeros((bp, ML_HEADS, ML_HD, ML_HD), f32), jnp.zeros((bp, ML_HEADS, ML_HD), f32),
            jnp.zeros((bp, 1, 128), f32), jnp.zeros((bp, HG_HEADS, HG_HD, HG_HD), f32))
    y_prompt, p_outs = _run_trunk(x_prompt, [zero] * depth, lp, nf, consts, 256)
    s_states = [_states_to_kernel(state_conv[l], state_ssd[l], state_mlstm_c[l], state_mlstm_n[l],
                                  state_mlstm_m[l], state_hgrn[l]) for l in range(depth)]
    y_sample, s_outs = _run_trunk(x_sample, s_states, lp, nf, consts, 256)
    return (y_prompt, y_sample) + _states_from_kernel(p_outs) + _states_from_kernel(s_outs)
```

```python
import functools

import numpy as np
import jax
import jax.numpy as jnp
from jax import lax
from jax.experimental import pallas as pl
from jax.experimental.pallas import tpu as pltpu

D_MODEL = 1024
CHUNK = 64
SSD_WIDTH = 1024
SSD_HEADDIM = 64
SSD_HEADS = 16
SSD_GROUPS = 2
SSD_STATE = 128
CONV_W = 4
CONV_CH = SSD_WIDTH + 2 * SSD_GROUPS * SSD_STATE
ML_WIDTH = 512
ML_HEADS = 4
ML_HD = 128
HG_WIDTH = 512
HG_HEADS = 4
HG_HD = 128
D_MIX = 2048
D_FF = 2816
EPS = 1e-6

C_Z = 0
C_XBC = 1024
C_MQ, C_MK, C_MV, C_MO = 2560, 3072, 3584, 4096
C_HQ, C_HK, C_HI, C_HG = 4608, 5120, 5632, 6144
C_SM = 6656
W_COLS = 6784
C_HL = 6784
PROJ_COLS = 7296
TIME_BLOCK = 256
LOG2E = 1.4426950408889634
LANE_DT = 0
LANE_MI = 16
LANE_MF = 20

VMEM_LIMIT = 52 * 1024 * 1024
NEG_INF = float("-inf")

HG_LEVELS = (32, 16, 8, 4, 2, 1)

_NT = (((1,), (1,)), ((), ()))


def _bf(x):
    return x.astype(jnp.bfloat16)


def _dot(a, b):
    return jnp.dot(a, b, preferred_element_type=jnp.float32)


def _dot_nt(a, b):
    return lax.dot_general(a, b, _NT, preferred_element_type=jnp.float32)


def _sigmoid(x):
    return 1.0 / (1.0 + jnp.exp(-x))


def _silu(x):
    return x * _sigmoid(x)


def _softplus(x):
    return jnp.maximum(x, 0.0) + jnp.log(1.0 + jnp.exp(-jnp.abs(x)))


def _split3(x):
    hi = _bf(x)
    r1 = x - hi.astype(jnp.float32)
    mid = _bf(r1)
    r2 = r1 - mid.astype(jnp.float32)
    return hi, mid, _bf(r2)


def _rms(x, gain):
    return x * lax.rsqrt(jnp.mean(x * x, axis=-1, keepdims=True) + EPS) * gain


def _hg_lower_bound(lb_all, layer):
    lb_e = jnp.exp(lb_all - jnp.max(lb_all, axis=0, keepdims=True))
    lb_soft = lb_e * (1.0 / jnp.sum(lb_e, axis=0, keepdims=True))
    return jnp.sum(lb_soft[0:layer + 1], axis=0, keepdims=True) - lb_soft[0:1]


_PROJ_GROUPS = (
    (C_Z, 512, "silu"), (C_Z + 512, 512, "silu"),
    (C_XBC, 512, None), (C_XBC + 512, 512, None), (C_XBC + 1024, 512, None),
    (C_MQ, 512, None), (C_MK, 512, "kscale"), (C_MV, 512, None), (C_MO, 512, "sigmoid"),
    (C_HQ, 512, "silu"), (C_HK, 512, "fgate"), (C_HI, 512, None), (C_HG, 512, "silu"),
    (C_SM, 128, None),
)


def _in_proj_kernel(x_ref, g_ref, w_ref, lb_ref, o_ref, *, layer):
    h = _bf(_rms(x_ref[...], g_ref[...]))
    for c0, width, act in _PROJ_GROUPS:
        acc = _dot(h, w_ref[:, c0:c0 + width])
        if act == "silu":
            acc = _silu(acc)
        elif act == "sigmoid":
            acc = _sigmoid(acc)
        elif act == "kscale":
            acc = acc * (ML_HD ** -0.5)
        elif act == "fgate":
            lb = _hg_lower_bound(lb_ref[...], layer)
            fg = lb + (1.0 - lb) * _sigmoid(acc)
            o_ref[:, C_HL:C_HL + width] = jnp.log2(fg)
            acc = 1.0 - fg
        o_ref[:, c0:c0 + width] = acc


def _in_proj(x2d, gain, w, hg_lb, tm, layer):
    n = x2d.shape[0]
    return pl.pallas_call(
        functools.partial(_in_proj_kernel, layer=layer),
        grid=(n // tm,),
        in_specs=[
            pl.BlockSpec((tm, D_MODEL), lambda i: (i, 0)),
            pl.BlockSpec((1, D_MODEL), lambda i: (0, 0)),
            pl.BlockSpec((D_MODEL, W_COLS), lambda i: (0, 0), pipeline_mode=pl.Buffered(1)),
            pl.BlockSpec(hg_lb.shape, lambda i: (0, 0)),
        ],
        out_specs=pl.BlockSpec((tm, PROJ_COLS), lambda i: (i, 0)),
        out_shape=jax.ShapeDtypeStruct((n, PROJ_COLS), jnp.float32),
        compiler_params=pltpu.CompilerParams(
            dimension_semantics=("parallel",), vmem_limit_bytes=VMEM_LIMIT),
        name="in_proj",
    )(x2d, gain, w, hg_lb)


FF_TILE = 256


def _out_ffn_kernel(x_ref, mix_ref, wo_ref, gn_ref, wi_ref, wf_ref, gf_ref, o_ref, *, final_norm):
    x1 = x_ref[...] + _dot(mix_ref[...], wo_ref[...])
    h2 = _bf(_rms(x1, gn_ref[...]))
    acc = x1
    for j in range(D_FF // FF_TILE):
        g = _dot(h2, wi_ref[:, j * FF_TILE:(j + 1) * FF_TILE])
        u = _dot(h2, wi_ref[:, D_FF + j * FF_TILE:D_FF + (j + 1) * FF_TILE])
        acc = acc + _dot(_bf(_silu(g) * u), wf_ref[j * FF_TILE:(j + 1) * FF_TILE, :])
    if final_norm:
        acc = _rms(acc, gf_ref[...])
    o_ref[...] = acc


def _out_ffn(x2d, mix2d, w_out, g_ffn, w_fi, w_fo, g_final, tm, final_norm):
    n = x2d.shape[0]
    const = lambda i: (0, 0)
    one = pl.Buffered(1)
    return pl.pallas_call(
        functools.partial(_out_ffn_kernel, final_norm=final_norm),
        grid=(n // tm,),
        in_specs=[
            pl.BlockSpec((tm, D_MODEL), lambda i: (i, 0)),
            pl.BlockSpec((tm, D_MIX), lambda i: (i, 0)),
            pl.BlockSpec((D_MIX, D_MODEL), const, pipeline_mode=one),
            pl.BlockSpec((1, D_MODEL), const),
            pl.BlockSpec((D_MODEL, 2 * D_FF), const, pipeline_mode=one),
            pl.BlockSpec((D_FF, D_MODEL), const, pipeline_mode=one),
            pl.BlockSpec((1, D_MODEL), const),
        ],
        out_specs=pl.BlockSpec((tm, D_MODEL), lambda i: (i, 0)),
        out_shape=jax.ShapeDtypeStruct((n, D_MODEL), jnp.float32),
        compiler_params=pltpu.CompilerParams(
            dimension_semantics=("parallel",), vmem_limit_bytes=VMEM_LIMIT),
        name="out_ffn",
    )(x2d, mix2d, w_out, g_ffn, w_fi, w_fo, g_final)


def _mixer_constants():
    L = CHUNK
    e = np.zeros((128, SSD_WIDTH), np.float32)
    for h in range(SSD_HEADS):
        e[h, h * SSD_HEADDIM:(h + 1) * SSD_HEADDIM] = 1.0
    e3 = np.concatenate([e, e, e], axis=0)

    l = np.arange(L)[:, None]
    c = np.arange(SSD_WIDTH)[None, :]
    dsel = ((c % L) == l).astype(np.float32)
    s2 = np.arange(128)[None, :] % L
    causal2 = (s2 <= l).astype(np.float32)

    t = np.arange(L)
    blocks = []
    masks = []
    isq = np.zeros((L, 128), np.float32)
    for li, m in enumerate(HG_LEVELS):
        cm = np.zeros((L, L), np.float32)
        start = (t // (2 * m)) * (2 * m)
        mid = start + m - 1
        query = (t - start) >= m
        for s in range(L):
            if query[s]:
                cm[s, mid[s] + 1:s + 1] = 1.0
            else:
                cm[s, s + 1:mid[s] + 1] = 1.0
        blocks.append(cm)
        same = (start[:, None] == start[None, :])
        mk = same & query[:, None] & (~query[None, :])
        masks.append(np.concatenate([mk, mk], axis=1).astype(np.float32))
        isq[:, li] = query.astype(np.float32)
    tri = (t[None, :] <= t[:, None]).astype(np.float32)
    blocks.append(tri)
    blocks.append((t[None, :] > t[:, None]).astype(np.float32))
    ce = np.concatenate(blocks, axis=0)
    ce3 = np.concatenate([ce, ce, ce], axis=1)
    diag = (t[:, None] == t[None, :])
    masks.append(np.concatenate([diag, diag], axis=1).astype(np.float32))
    hmask = np.stack(masks, axis=0)
    return (jnp.asarray(e3, jnp.bfloat16), jnp.asarray(dsel), jnp.asarray(causal2),
            jnp.asarray(ce3, jnp.bfloat16), jnp.asarray(hmask), jnp.asarray(isq))


def _scan_time(x, op, identity, row):
    sh = 1
    while sh < x.shape[0]:
        r = pltpu.roll(x, sh, axis=0)
        x = op(x, jnp.where(row >= sh, r, identity))
        sh *= 2
    return x


def _blockdiag_rows(a, b):
    z = jnp.zeros_like(a)
    return jnp.concatenate([jnp.concatenate([a, z], axis=1), jnp.concatenate([z, b], axis=1)], axis=0)


def _mixer_chunk(col, put, sp_ref, cw_ref, cb_ref, dsk_ref, sg_ref, mlg_ref, hgg_ref,
                 e3_ref, dsel_ref, causal_ref, ce3_ref, hmask_ref, isq_ref,
                 mc_ref, mn_ref, mm_ref, ht_ref, hgt_ref, cbuf_ref, *, t_valid):
    L = CHUNK
    row = lax.broadcasted_iota(jnp.int32, (L, 128), 0)
    lane = lax.broadcasted_iota(jnp.int32, (L, 128), 1)
    lo_half = lane < 64
    causal2 = causal_ref[...] > 0.5

    sm = col(C_SM, 128)
    dt = _softplus(sm + sp_ref[0:1, :])
    a_neg2 = jnp.where(lane[0:1] < SSD_HEADS, -LOG2E * jnp.exp(sp_ref[1:2, :]), 0.0)
    ig = pltpu.roll(sm, LANE_MF - LANE_MI, axis=1) + sp_ref[2:3, :]
    lf = -_softplus(-(sm + sp_ref[3:4, :]))
    if t_valid < L:
        ok = row < t_valid
        dt = jnp.where(ok, dt, 0.0)
        ig = jnp.where(ok, ig, NEG_INF)
        lf = jnp.where(ok, lf, 0.0)
    ml_lane = (lane >= LANE_MF) & (lane < LANE_MF + ML_HEADS)
    cs = _scan_time(jnp.where(lane < SSD_HEADS, dt * a_neg2, jnp.where(ml_lane, lf, 0.0)), jnp.add, 0.0, row)
    acum2 = cs
    u = jnp.where(ml_lane, ig - cs, 0.0)
    m_prev = mm_ref[0, 0]
    m_run = jnp.maximum(_scan_time(u, jnp.maximum, NEG_INF, row), m_prev)
    m_i = cs + m_run
    m_last = m_run[L - 1:L, :]
    w_inter = jnp.exp(m_prev - m_run)
    inv_floor = jnp.exp(-m_i)
    wk_s = jnp.exp(u - m_last)
    sc_s = jnp.exp(m_prev - m_last)

    stack = jnp.concatenate([dt, acum2, jnp.exp2(acum2), jnp.exp2(acum2[L - 1:L, :] - acum2)], axis=0)
    hi, mid, lo = _split3(stack)
    ex = _dot(jnp.concatenate([hi, mid, lo], axis=1), e3_ref[...])
    dt_f, ac_f, eac_f, dend_f = ex[0:L], ex[L:2 * L], ex[2 * L:3 * L], ex[3 * L:4 * L]

    xbc = col(C_XBC, CONV_CH)
    cbuf_ref[8:8 + L, :] = xbc
    conv = cb_ref[...] + xbc * cw_ref[3:4, :]
    for w in range(CONV_W - 1):
        conv = conv + cbuf_ref[5 + w:5 + w + L, :] * cw_ref[w:w + 1, :]
    cbuf_ref[5:8, :] = xbc[t_valid - 3:t_valid, :]
    xbc_a = _silu(conv)
    xs = xbc_a[:, 0:SSD_WIDTH]
    xt = xs * dt_f
    ctr = jnp.sum(jnp.where(dsel_ref[...] > 0.5, ac_f, 0.0), axis=0, keepdims=True)
    xdec = _bf(xt * dend_f)
    eac_end = eac_f[L - 1:L, :]

    y_parts = []
    for g in range(SSD_GROUPS):
        bm = xbc_a[:, SSD_WIDTH + g * SSD_STATE:SSD_WIDTH + (g + 1) * SSD_STATE]
        cm = xbc_a[:, SSD_WIDTH + (SSD_GROUPS + g) * SSD_STATE:SSD_WIDTH + (SSD_GROUPS + g + 1) * SSD_STATE]
        bm_b, cm_b = _bf(bm), _bf(cm)
        cb2 = _dot_nt(cm_b, jnp.concatenate([bm_b, bm_b], axis=0))
        gs = slice(g * 512, (g + 1) * 512)
        y_inter = _dot(cm_b, _bf(ht_ref[:, gs]))
        for pp in range(4):
            p = g * 4 + pp
            ps = slice(p * 128, (p + 1) * 128)
            seg = ac_f[:, ps] - ctr[:, ps]
            m2 = _bf(cb2 * jnp.exp2(jnp.where(causal2, seg, NEG_INF)))
            xp = xt[:, ps]
            xb = _bf(jnp.concatenate([jnp.where(lo_half, xp, 0.0), jnp.where(lo_half, 0.0, xp)], axis=0))
            y_parts.append(_dot(m2, xb) + eac_f[:, ps] * y_inter[:, pp * 128:(pp + 1) * 128])
        ht_ref[:, gs] = ht_ref[:, gs] * eac_end[:, gs] + _dot(_bf(bm.T), xdec[:, gs])
    y = jnp.concatenate(y_parts, axis=1) + dsk_ref[...] * xs
    y_ssd = _rms(y * col(C_Z, SSD_WIDTH), sg_ref[...])
    put(0, _bf(y_ssd))

    dsel128 = dsel_ref[:, 0:128] > 0.5
    for pp in range(ML_HEADS // 2):
        h0, h1 = 2 * pp, 2 * pp + 1
        l0, l1 = LANE_MF + h0, LANE_MF + h1
        m_row = jnp.where(lo_half, m_run[:, l0:l0 + 1], m_run[:, l1:l1 + 1])
        u_row = jnp.where(lo_half, u[:, l0:l0 + 1], u[:, l1:l1 + 1])
        u_col = jnp.sum(jnp.where(dsel128, u_row, 0.0), axis=0, keepdims=True)
        wgt = jnp.exp(jnp.where(causal2, u_col - m_row, NEG_INF))
        q2 = _bf(col(C_MQ + pp * 256, 256))
        k0, k1 = col(C_MK + h0 * 128, 128), col(C_MK + h1 * 128, 128)
        v0, v1 = col(C_MV + h0 * 128, 128), col(C_MV + h1 * 128, 128)
        s2 = _dot_nt(q2, _bf(_blockdiag_rows(k0, k1))) * wgt
        num2 = _dot(_bf(s2), _bf(_blockdiag_rows(v0, v1)))
        dens = (jnp.sum(jnp.where(lo_half, s2, 0.0), axis=-1, keepdims=True),
                jnp.sum(jnp.where(lo_half, 0.0, s2), axis=-1, keepdims=True))
        for j, (h, kh, vh) in enumerate(((h0, k0, v0), (h1, k1, v1))):
            ln = LANE_MF + h
            qh = col(C_MQ + h * 128, 128)
            wi = w_inter[:, ln:ln + 1]
            num = num2[:, j * 128:(j + 1) * 128] + wi * _dot(_bf(qh), _bf(mc_ref[0, 0, h]))
            qn = jnp.sum(qh * mn_ref[0, 0, h:h + 1, :], axis=-1, keepdims=True)
            den = dens[j] + wi * qn
            hval = num * (1.0 / jnp.maximum(jnp.abs(den), inv_floor[:, ln:ln + 1]))
            hn = _rms(hval, mlg_ref[:, h * 128:(h + 1) * 128])
            put(SSD_WIDTH + h * 128, _bf(col(C_MO + h * 128, 128) * hn))
            kw = kh * wk_s[:, ln:ln + 1]
            sc = sc_s[:, ln:ln + 1]
            mc_ref[0, 0, h] = sc * mc_ref[0, 0, h] + _dot(_bf(kw.T), _bf(vh))
            mn_ref[0, 0, h:h + 1, :] = sc * mn_ref[0, 0, h:h + 1, :] + jnp.sum(kw, axis=0, keepdims=True)
    mm_ref[0, 0] = jnp.where(ml_lane[0:1], m_i[L - 1:L, :], 0.0)

    kk = col(C_HK, HG_WIDTH)
    glog2 = col(C_HL, HG_WIDTH)
    if t_valid < L:
        glog2 = jnp.where(lax.broadcasted_iota(jnp.int32, (L, HG_WIDTH), 0) < t_valid, glog2, 0.0)
    qq = col(C_HQ, HG_WIDTH)
    vv = col(C_HI, HG_WIDTH)
    ghi, gmid, glo = _split3(glog2)
    exps = _dot(ce3_ref[...], jnp.concatenate([ghi, gmid, glo], axis=0))
    nl = len(HG_LEVELS)
    qk_lv = []
    for li in range(nl):
        isq = isq_ref[:, li:li + 1] > 0.5
        qk_lv.append(_bf(jnp.where(isq, qq, kk) * jnp.exp2(exps[li * L:(li + 1) * L])))
    gc = exps[nl * L:(nl + 1) * L]
    gend = exps[(nl + 1) * L:(nl + 2) * L]
    q_dec = _bf(qq * jnp.exp2(gc))
    k_dec = _bf(kk * jnp.exp2(gend))
    s_dec = jnp.exp2(gc[L - 1:L, :])
    qq_b, kk_b, vv_b = _bf(qq), _bf(kk), _bf(vv)

    for pp in range(HG_HEADS // 2):
        h0, h1 = 2 * pp, 2 * pp + 1
        s0, s1 = slice(h0 * 128, (h0 + 1) * 128), slice(h1 * 128, (h1 + 1) * 128)
        pr = slice(pp * 256, (pp + 1) * 256)
        a2 = jnp.where(hmask_ref[nl] > 0.5, _dot_nt(qq_b[:, pr], _blockdiag_rows(kk_b[:, s0], kk_b[:, s1])), 0.0)
        for li in range(nl):
            x = qk_lv[li]
            a_l = _dot_nt(x[:, pr], _blockdiag_rows(x[:, s0], x[:, s1]))
            a2 = jnp.where(hmask_ref[li] > 0.5, a_l, a2)
        o2 = _dot(_bf(a2), _blockdiag_rows(vv_b[:, s0], vv_b[:, s1]))
        for j, (h, hs) in enumerate(((h0, s0), (h1, s1))):
            st = hgt_ref[h]
            o = o2[:, j * 128:(j + 1) * 128] + _dot_nt(q_dec[:, hs], _bf(st))
            on = _rms(o, hgg_ref[:, hs])
            put(SSD_WIDTH + ML_WIDTH + h * 128, _bf(on * col(C_HG + h * 128, 128)))
            hgt_ref[h] = st * s_dec[:, hs] + _dot(_bf(vv[:, hs].T), k_dec[:, hs])


def _mixer_kernel(*refs, t_rows, n_chunks, has_state, has_prev):
    it = iter(refs)
    proj_ref = next(it)
    state0 = [next(it) for _ in range(6)] if has_state else None
    sp_ref, cw_ref, cb_ref, dsk_ref, sg_ref, mlg_ref, hgg_ref = (next(it) for _ in range(7))
    e3_ref, dsel_ref, causal_ref, ce3_ref, hmask_ref, isq_ref = (next(it) for _ in range(6))
    if has_prev:
        for _ in range(6):
            next(it)
    mix_ref, conv_ref, ssd_ref, mc_ref, mn_ref, mm_ref, hg_ref = (next(it) for _ in range(7))
    cbuf_ref, ht_ref, hgt_ref = (next(it) for _ in range(3))
    padded = t_rows < CHUNK
    pbuf_ref = next(it) if padded else None
    tb = pl.program_id(1)

    @pl.when(tb == 0)
    def _init():
        if has_state:
            conv0_ref, ssd0_ref, mc0_ref, mn0_ref, mm0_ref, hg0_ref = state0
            cbuf_ref[5:8, :] = conv0_ref[0, 0]
            ht_ref[...] = ssd0_ref[0, 0].reshape(SSD_WIDTH, SSD_STATE).T
            mc_ref[...] = mc0_ref[...]
            mn_ref[...] = mn0_ref[...]
            mm_ref[...] = mm0_ref[...]
            for h in range(HG_HEADS):
                hgt_ref[h] = hg0_ref[0, 0, h].T
        else:
            cbuf_ref[5:8, :] = jnp.zeros((CONV_W - 1, CONV_CH), jnp.float32)
            ht_ref[...] = jnp.zeros_like(ht_ref)
            mc_ref[...] = jnp.zeros_like(mc_ref)
            mn_ref[...] = jnp.zeros_like(mn_ref)
            mm_ref[...] = jnp.zeros_like(mm_ref)
            hgt_ref[...] = jnp.zeros_like(hgt_ref)

    step = functools.partial(
        _mixer_chunk, sp_ref=sp_ref, cw_ref=cw_ref, cb_ref=cb_ref, dsk_ref=dsk_ref, sg_ref=sg_ref,
        mlg_ref=mlg_ref, hgg_ref=hgg_ref, e3_ref=e3_ref, dsel_ref=dsel_ref, causal_ref=causal_ref,
        ce3_ref=ce3_ref, hmask_ref=hmask_ref, isq_ref=isq_ref, mc_ref=mc_ref, mn_ref=mn_ref, mm_ref=mm_ref,
        ht_ref=ht_ref, hgt_ref=hgt_ref, cbuf_ref=cbuf_ref, t_valid=t_rows if padded else CHUNK)

    if padded:
        pbuf_ref[0:t_rows, :] = proj_ref[0]
        pbuf_ref[t_rows:CHUNK, :] = jnp.zeros((CHUNK - t_rows, PROJ_COLS), jnp.float32)

        def put(c0, val):
            mix_ref[0, :, c0:c0 + val.shape[1]] = val[0:t_rows]

        step(lambda c0, width: pbuf_ref[:, c0:c0 + width], put)
    else:
        def body(ci, carry):
            rows = pl.ds(pl.multiple_of(ci * CHUNK, CHUNK), CHUNK)

            def put(c0, val):
                mix_ref[0, rows, c0:c0 + val.shape[1]] = val

            step(lambda c0, width: proj_ref[0, rows, c0:c0 + width], put)
            return carry
        lax.fori_loop(0, n_chunks, body, 0, unroll=2 if n_chunks % 2 == 0 else 1)

    @pl.when(tb == pl.num_programs(1) - 1)
    def _fin():
        conv_ref[0, 0] = cbuf_ref[5:8, :]
        ssd_ref[0, 0] = ht_ref[...].T.reshape(SSD_HEADS, SSD_HEADDIM, SSD_STATE)
        for h in range(HG_HEADS):
            hg_ref[0, 0, h] = hgt_ref[h].T


def _mixer(proj, state_in, prev, params, consts, layer, depth):
    b, t, _ = proj.shape
    if t < CHUNK:
        tblk = t
    else:
        tblk = min(TIME_BLOCK, t)
        assert t % tblk == 0 and tblk % CHUNK == 0
    lyr4 = lambda i, j: (layer, i, 0, 0)
    lyr5 = lambda i, j: (layer, i, 0, 0, 0)
    c2 = lambda i, j: (0, 0)
    c3 = lambda i, j: (0, 0, 0)
    state_specs = [
        pl.BlockSpec((1, 1, CONV_W - 1, CONV_CH), lyr4),
        pl.BlockSpec((1, 1, SSD_HEADS, SSD_HEADDIM, SSD_STATE), lyr5),
        pl.BlockSpec((1, 1, ML_HEADS, ML_HD, ML_HD), lyr5),
        pl.BlockSpec((1, 1, ML_HEADS, ML_HD), lyr4),
        pl.BlockSpec((1, 1, 1, 128), lyr4),
        pl.BlockSpec((1, 1, HG_HEADS, HG_HD, HG_HD), lyr5),
    ]
    state_shapes = [
        jax.ShapeDtypeStruct((depth, b, CONV_W - 1, CONV_CH), jnp.float32),
        jax.ShapeDtypeStruct((depth, b, SSD_HEADS, SSD_HEADDIM, SSD_STATE), jnp.float32),
        jax.ShapeDtypeStruct((depth, b, ML_HEADS, ML_HD, ML_HD), jnp.float32),
        jax.ShapeDtypeStruct((depth, b, ML_HEADS, ML_HD), jnp.float32),
        jax.ShapeDtypeStruct((depth, b, 1, 128), jnp.float32),
        jax.ShapeDtypeStruct((depth, b, HG_HEADS, HG_HD, HG_HD), jnp.float32),
    ]
    args = [proj]
    in_specs = [pl.BlockSpec((1, tblk, PROJ_COLS), lambda i, j: (i, j, 0))]
    if state_in is not None:
        args += list(state_in)
        in_specs += state_specs
    args += list(params) + list(consts)
    in_specs += [pl.BlockSpec(a.shape, c3 if a.ndim == 3 else c2) for a in list(params) + list(consts)]
    aliases = {}
    if prev is not None:
        for k, a in enumerate(prev):
            aliases[len(args)] = 1 + k
            args.append(a)
            in_specs.append(pl.BlockSpec(memory_space=pl.ANY))
    scratch = [pltpu.VMEM((8 + CHUNK, CONV_CH), jnp.float32),
               pltpu.VMEM((SSD_STATE, SSD_WIDTH), jnp.float32),
               pltpu.VMEM((HG_HEADS, HG_HD, HG_HD), jnp.float32)]
    if t < CHUNK:
        scratch.append(pltpu.VMEM((CHUNK, PROJ_COLS), jnp.float32))
    return pl.pallas_call(
        functools.partial(_mixer_kernel, t_rows=tblk, n_chunks=max(tblk // CHUNK, 1),
                          has_state=state_in is not None, has_prev=prev is not None),
        grid=(b, t // tblk),
        in_specs=in_specs,
        out_specs=[pl.BlockSpec((1, tblk, D_MIX), lambda i, j: (i, j, 0))] + state_specs,
        out_shape=[jax.ShapeDtypeStruct((b, t, D_MIX), jnp.bfloat16)] + state_shapes,
        scratch_shapes=scratch,
        input_output_aliases=aliases,
        compiler_params=pltpu.CompilerParams(
            dimension_semantics=("parallel", "arbitrary"), vmem_limit_bytes=VMEM_LIMIT),
        name="mixer",
    )(*args)


def _regroup_w_in(w):
    o = {}
    acc = 0
    for name, size in (("z", 1024), ("xbc", 1536), ("dt", 16), ("mq", 512), ("mk", 512), ("mv", 512),
                       ("mi", 4), ("mf", 4), ("mo", 512), ("hq", 512), ("hf", 512), ("hi", 512), ("hg", 512)):
        o[name] = w[:, acc:acc + size]
        acc += size
    pad = jnp.zeros((w.shape[0], W_COLS - C_SM - 24), w.dtype)
    cols = [o["z"], o["xbc"], o["mq"], o["mk"], o["mv"], o["mo"], o["hq"], o["hf"], o["hi"], o["hg"],
            o["dt"], o["mi"], o["mf"], pad]
    return jnp.concatenate(cols, axis=1).astype(jnp.bfloat16)


def _small_params(dt_bias, a_log, ml_bi, ml_bf):
    sp = jnp.zeros((8, 128), jnp.float32)
    sp = sp.at[0, LANE_DT:LANE_DT + SSD_HEADS].set(dt_bias)
    sp = sp.at[1, LANE_DT:LANE_DT + SSD_HEADS].set(a_log)
    sp = sp.at[2, LANE_MF:LANE_MF + ML_HEADS].set(ml_bi)
    sp = sp.at[3, LANE_MF:LANE_MF + ML_HEADS].set(ml_bf)
    return sp


def _run_trunk(x, state_in, lp, norm_final, consts, tm_proj, tm_ffn):
    b, t, _ = x.shape
    assert t % CHUNK == 0 or t < CHUNK, "a partial chunk is only supported for single-chunk sequences"
    depth = len(lp)
    x2d = x.reshape(b * t, D_MODEL)
    states = None
    for l in range(depth):
        p = lp[l]
        proj = _in_proj(x2d, p["norm_mix"], p["w_in"], p["hg_lb"], tm_proj, l).reshape(b, t, PROJ_COLS)
        params = (p["sp"], p["conv_w"], p["conv_b"], p["d_skip"], p["ssd_gain"], p["ml_gain"], p["hg_gain"])
        res = _mixer(proj, state_in, states, params, consts, l, depth)
        states = res[1:]
        x2d = _out_ffn(x2d, res[0].reshape(b * t, D_MIX), p["w_out"], p["norm_ffn"], p["w_ffn_in"],
                       p["w_ffn_out"], norm_final, tm_ffn, l == depth - 1)
    conv, ssd, mc, mn, mm_p, hg = states
    return x2d.reshape(b, t, D_MODEL), (conv, ssd, mc, mn, mm_p[:, :, 0, LANE_MF:LANE_MF + ML_HEADS], hg)


def kernel(x_prompt, x_sample, state_conv, state_ssd, state_mlstm_c, state_mlstm_n, state_mlstm_m, state_hgrn,
           norm_mix, w_in, conv_w, conv_b, dt_bias, a_log, d_skip, ssd_gain, ml_bi, ml_bf, ml_gain,
           hg_lb, hg_gain, w_out, norm_ffn, w_ffn_in, w_ffn_out, norm_final):
    depth = w_in.shape[0]
    f32 = jnp.float32
    consts = _mixer_constants()
    lp = []
    for l in range(depth):
        lp.append({
            "norm_mix": norm_mix[l].reshape(1, D_MODEL),
            "w_in": _regroup_w_in(w_in[l]),
            "sp": _small_params(dt_bias[l], a_log[l], ml_bi[l], ml_bf[l]),
            "conv_w": conv_w[l],
            "conv_b": conv_b[l].reshape(1, CONV_CH),
            "d_skip": jnp.repeat(d_skip[l], SSD_HEADDIM).reshape(1, SSD_WIDTH),
            "ssd_gain": ssd_gain[l].reshape(1, SSD_WIDTH),
            "ml_gain": ml_gain[l].reshape(1, ML_WIDTH),
            "hg_lb": hg_lb.astype(f32),
            "hg_gain": hg_gain[l].reshape(1, HG_WIDTH),
            "w_out": w_out[l].astype(jnp.bfloat16),
            "norm_ffn": norm_ffn[l].reshape(1, D_MODEL),
            "w_ffn_in": w_ffn_in[l].astype(jnp.bfloat16),
            "w_ffn_out": w_ffn_out[l].astype(jnp.bfloat16),
        })
    nf = norm_final.reshape(1, D_MODEL)

    y_prompt, p_outs = _run_trunk(x_prompt, None, lp, nf, consts, 256, 512)
    bs = x_sample.shape[0]
    mm_p = jnp.zeros((depth, bs, 1, 128), f32).at[:, :, 0, LANE_MF:LANE_MF + ML_HEADS].set(state_mlstm_m)
    s_in = (state_conv, state_ssd, state_mlstm_c, state_mlstm_n, mm_p, state_hgrn)
    y_sample, s_outs = _run_trunk(x_sample, s_in, lp, nf, consts, 256, 512)
    return (y_prompt, y_sample) + p_outs + s_outs
```

```python
import functools

import numpy as np
import jax
import jax.numpy as jnp
from jax import lax
from jax.experimental import pallas as pl
from jax.experimental.pallas import tpu as pltpu

D_MODEL = 1024
CHUNK = 64
SSD_WIDTH = 1024
SSD_HEADDIM = 64
SSD_HEADS = 16
SSD_GROUPS = 2
SSD_STATE = 128
CONV_W = 4
CONV_CH = SSD_WIDTH + 2 * SSD_GROUPS * SSD_STATE
ML_WIDTH = 512
ML_HEADS = 4
ML_HD = 128
HG_WIDTH = 512
HG_HEADS = 4
HG_HD = 128
D_MIX = 2048
D_FF = 2816
EPS = 1e-6

C_Z = 0
C_XBC = 1024
C_MQ, C_MK, C_MV, C_MO = 2560, 3072, 3584, 4096
C_HQ, C_HK, C_HI, C_HG = 4608, 5120, 5632, 6144
C_SM = 6656
W_COLS = 6784
C_HL = 6784
PROJ_COLS = 7296
TIME_BLOCK = 256
LOG2E = 1.4426950408889634
LANE_DT = 0
LANE_MI = 16
LANE_MF = 20

VMEM_LIMIT = 52 * 1024 * 1024
NEG_INF = float("-inf")

HG_LEVELS = (32, 16, 8, 4, 2, 1)

_NT = (((1,), (1,)), ((), ()))


def _bf(x):
    return x.astype(jnp.bfloat16)


def _dot(a, b):
    return jnp.dot(a, b, preferred_element_type=jnp.float32)


def _dot_nt(a, b):
    return lax.dot_general(a, b, _NT, preferred_element_type=jnp.float32)


def _sigmoid(x):
    return 1.0 / (1.0 + jnp.exp(-x))


def _silu(x):
    return x * _sigmoid(x)


def _softplus(x):
    return jnp.maximum(x, 0.0) + jnp.log(1.0 + jnp.exp(-jnp.abs(x)))


def _split3(x):
    hi = _bf(x)
    r1 = x - hi.astype(jnp.float32)
    mid = _bf(r1)
    r2 = r1 - mid.astype(jnp.float32)
    return hi, mid, _bf(r2)


def _rms(x, gain):
    return x * lax.rsqrt(jnp.mean(x * x, axis=-1, keepdims=True) + EPS) * gain


def _hg_lower_bound(lb_all, layer):
    lb_e = jnp.exp(lb_all - jnp.max(lb_all, axis=0, keepdims=True))
    lb_soft = lb_e * (1.0 / jnp.sum(lb_e, axis=0, keepdims=True))
    return jnp.sum(lb_soft[0:layer + 1], axis=0, keepdims=True) - lb_soft[0:1]


def _causal_conv(cbuf_ref, cw_ref, cb_ref, rows):
    acc = cb_ref[...] + cbuf_ref[8:8 + rows, :] * cw_ref[CONV_W - 1:CONV_W, :]
    for w in range(CONV_W - 1):
        acc = acc + cbuf_ref[5 + w:5 + w + rows, :] * cw_ref[w:w + 1, :]
    return acc


_PROJ_GROUPS = (
    (C_Z, 512, "silu"), (C_Z + 512, 512, "silu"),
    (C_XBC, 512, None), (C_XBC + 512, 512, None), (C_XBC + 1024, 512, None),
    (C_MQ, 512, None), (C_MK, 512, "kscale"), (C_MV, 512, None), (C_MO, 512, "sigmoid"),
    (C_HQ, 512, "silu"), (C_HK, 512, "fgate"), (C_HI, 512, None), (C_HG, 512, "silu"),
    (C_SM, 128, None),
)


def _proj_group(h, w_ref, lb_ref, layer, group, store):
    c0, width, act = group
    acc = _dot(h, w_ref[:, c0:c0 + width])
    if act == "silu":
        acc = _silu(acc)
    elif act == "sigmoid":
        acc = _sigmoid(acc)
    elif act == "kscale":
        acc = acc * (ML_HD ** -0.5)
    elif act == "fgate":
        lb = _hg_lower_bound(lb_ref[...], layer)
        fg = lb + (1.0 - lb) * _sigmoid(acc)
        store(C_HL, jnp.log2(fg))
        acc = 1.0 - fg
    store(c0, acc)


def _in_proj_kernel(x_ref, g_ref, w_ref, lb_ref, o_ref, *, layer):
    h = _bf(_rms(x_ref[...], g_ref[...]))

    def store(c0, val):
        o_ref[:, c0:c0 + val.shape[1]] = val

    for group in _PROJ_GROUPS:
        _proj_group(h, w_ref, lb_ref, layer, group, store)


def _in_proj(x2d, gain, w, hg_lb, tm, layer):
    n = x2d.shape[0]
    c2 = lambda i: (0, 0)
    return pl.pallas_call(
        functools.partial(_in_proj_kernel, layer=layer),
        grid=(n // tm,),
        in_specs=[
            pl.BlockSpec((tm, D_MODEL), lambda i: (i, 0)),
            pl.BlockSpec((1, D_MODEL), c2),
            pl.BlockSpec((D_MODEL, W_COLS), c2, pipeline_mode=pl.Buffered(1)),
            pl.BlockSpec(hg_lb.shape, c2),
        ],
        out_specs=pl.BlockSpec((tm, PROJ_COLS), lambda i: (i, 0)),
        out_shape=jax.ShapeDtypeStruct((n, PROJ_COLS), jnp.float32),
        compiler_params=pltpu.CompilerParams(
            dimension_semantics=("parallel",), vmem_limit_bytes=VMEM_LIMIT),
        name="in_proj",
    )(x2d, gain, w, hg_lb)


FF_TILE = 256


def _out_ffn_kernel(x_ref, mix_ref, wo_ref, gn_ref, wi_ref, wf_ref, gf_ref, o_ref, *, final_norm):
    x1 = x_ref[...] + _dot(mix_ref[...], wo_ref[...])
    h2 = _bf(_rms(x1, gn_ref[...]))
    acc = x1
    for j in range(D_FF // FF_TILE):
        g = _dot(h2, wi_ref[:, j * FF_TILE:(j + 1) * FF_TILE])
        u = _dot(h2, wi_ref[:, D_FF + j * FF_TILE:D_FF + (j + 1) * FF_TILE])
        acc = acc + _dot(_bf(_silu(g) * u), wf_ref[j * FF_TILE:(j + 1) * FF_TILE, :])
    if final_norm:
        acc = _rms(acc, gf_ref[...])
    o_ref[...] = acc


def _out_ffn(x2d, mix2d, w_out, g_ffn, w_fi, w_fo, g_final, tm, final_norm):
    n = x2d.shape[0]
    const = lambda i: (0, 0)
    one = pl.Buffered(1)
    return pl.pallas_call(
        functools.partial(_out_ffn_kernel, final_norm=final_norm),
        grid=(n // tm,),
        in_specs=[
            pl.BlockSpec((tm, D_MODEL), lambda i: (i, 0)),
            pl.BlockSpec((tm, D_MIX), lambda i: (i, 0)),
            pl.BlockSpec((D_MIX, D_MODEL), const, pipeline_mode=one),
            pl.BlockSpec((1, D_MODEL), const),
            pl.BlockSpec((D_MODEL, 2 * D_FF), const, pipeline_mode=one),
            pl.BlockSpec((D_FF, D_MODEL), const, pipeline_mode=one),
            pl.BlockSpec((1, D_MODEL), const),
        ],
        out_specs=pl.BlockSpec((tm, D_MODEL), lambda i: (i, 0)),
        out_shape=jax.ShapeDtypeStruct((n, D_MODEL), jnp.float32),
        compiler_params=pltpu.CompilerParams(
            dimension_semantics=("parallel",), vmem_limit_bytes=VMEM_LIMIT),
        name="out_ffn",
    )(x2d, mix2d, w_out, g_ffn, w_fi, w_fo, g_final)


def _mixer_constants():
    L = CHUNK
    e3 = np.zeros((128, SSD_WIDTH), np.float32)
    for piece in range(3):
        for h in range(SSD_HEADS):
            e3[piece * SSD_HEADS + h, h * SSD_HEADDIM:(h + 1) * SSD_HEADDIM] = 1.0

    l = np.arange(L)[:, None]
    c = np.arange(SSD_WIDTH)[None, :]
    dsel = ((c % L) == l).astype(np.float32)
    s2 = np.arange(128)[None, :] % L
    causal2 = (s2 <= l).astype(np.float32)

    t = np.arange(L)
    blocks = []
    masks = []
    isq = np.zeros((L, 128), np.float32)
    for li, m in enumerate(HG_LEVELS):
        cm = np.zeros((L, L), np.float32)
        start = (t // (2 * m)) * (2 * m)
        mid = start + m - 1
        query = (t - start) >= m
        for s in range(L):
            if query[s]:
                cm[s, mid[s] + 1:s + 1] = 1.0
            else:
                cm[s, s + 1:mid[s] + 1] = 1.0
        blocks.append(cm)
        same = (start[:, None] == start[None, :])
        mk = same & query[:, None] & (~query[None, :])
        masks.append(np.concatenate([mk, mk], axis=1).astype(np.float32))
        isq[:, li] = query.astype(np.float32)
    tri = (t[None, :] <= t[:, None]).astype(np.float32)
    blocks.append(tri)
    blocks.append((t[None, :] > t[:, None]).astype(np.float32))
    ce = np.concatenate(blocks, axis=0)
    ce3 = np.concatenate([ce, ce, ce], axis=1)
    diag = (t[:, None] == t[None, :])
    masks.append(np.concatenate([diag, diag], axis=1).astype(np.float32))
    hmask = np.stack(masks, axis=0)
    return (jnp.asarray(e3, jnp.bfloat16), jnp.asarray(dsel), jnp.asarray(causal2),
            jnp.asarray(ce3, jnp.bfloat16), jnp.asarray(hmask), jnp.asarray(isq))


def _scan_time(x, op, identity, row):
    sh = 1
    while sh < x.shape[0]:
        r = pltpu.roll(x, sh, axis=0)
        x = op(x, jnp.where(row >= sh, r, identity))
        sh *= 2
    return x


def _blockdiag_rows(a, b):
    z = jnp.zeros_like(a)
    return jnp.concatenate([jnp.concatenate([a, z], axis=1), jnp.concatenate([z, b], axis=1)], axis=0)


def _mixer_chunk(col, put, sp_ref, cw_ref, cb_ref, dsk_ref, sg_ref, mlg_ref, hgg_ref,
                 e3_ref, dsel_ref, causal_ref, ce3_ref, hmask_ref, isq_ref,
                 mc_ref, mn_ref, mm_ref, ht_ref, hgt_ref, cbuf_ref, *, t_valid):
    L = CHUNK
    row = lax.broadcasted_iota(jnp.int32, (L, 128), 0)
    lane = lax.broadcasted_iota(jnp.int32, (L, 128), 1)
    lo_half = lane < 64
    causal2 = causal_ref[...] > 0.5

    nl = len(HG_LEVELS)
    ml_pairs = range(ML_HEADS // 2)
    hg_pairs = range(HG_HEADS // 2)

    sm = col(C_SM, 128)
    dt = _softplus(sm + sp_ref[0:1, :])
    a_neg2 = jnp.where(lane[0:1] < SSD_HEADS, -LOG2E * jnp.exp(sp_ref[1:2, :]), 0.0)
    ig = pltpu.roll(sm, LANE_MF - LANE_MI, axis=1) + sp_ref[2:3, :]
    lf = -_softplus(-(sm + sp_ref[3:4, :]))
    if t_valid < L:
        ok = row < t_valid
        dt = jnp.where(ok, dt, 0.0)
        ig = jnp.where(ok, ig, NEG_INF)
        lf = jnp.where(ok, lf, 0.0)
    ml_lane = (lane >= LANE_MF) & (lane < LANE_MF + ML_HEADS)
    cs = _scan_time(jnp.where(lane < SSD_HEADS, dt * a_neg2, jnp.where(ml_lane, lf, 0.0)), jnp.add, 0.0, row)
    acum2 = cs
    u = jnp.where(ml_lane, ig - cs, 0.0)
    m_prev = mm_ref[0, 0]
    m_run = jnp.maximum(_scan_time(u, jnp.maximum, NEG_INF, row), m_prev)
    m_i = cs + m_run
    m_last = m_run[L - 1:L, :]
    w_inter = jnp.exp(m_prev - m_run)
    inv_floor = jnp.exp(-m_i)
    wk_s = jnp.exp(u - m_last)
    sc_s = jnp.exp(m_prev - m_last)

    hi, mid, lo = _split3(jnp.concatenate([dt, acum2], axis=0))
    lane2 = lax.broadcasted_iota(jnp.int32, (2 * L, 128), 1)
    packed = jnp.where(
        lane2 < SSD_HEADS, hi.astype(jnp.float32),
        jnp.where(lane2 < 2 * SSD_HEADS, pltpu.roll(mid.astype(jnp.float32), SSD_HEADS, axis=1),
                  jnp.where(lane2 < 3 * SSD_HEADS, pltpu.roll(lo.astype(jnp.float32), 2 * SSD_HEADS, axis=1), 0.0)))
    ex = _dot(_bf(packed), e3_ref[...])
    glog2 = col(C_HL, HG_WIDTH)
    if t_valid < L:
        glog2 = jnp.where(lax.broadcasted_iota(jnp.int32, (L, HG_WIDTH), 0) < t_valid, glog2, 0.0)
    ghi, gmid, glo = _split3(glog2)
    exps = _dot(ce3_ref[...], jnp.concatenate([ghi, gmid, glo], axis=0))

    xbc = col(C_XBC, CONV_CH)
    cbuf_ref[8:8 + L, :] = xbc
    xbc_a = _silu(_causal_conv(cbuf_ref, cw_ref, cb_ref, L))
    cbuf_ref[5:8, :] = xbc[t_valid - 3:t_valid, :]
    xs = xbc_a[:, 0:SSD_WIDTH]

    ml_q2 = [_bf(col(C_MQ + pp * 256, 256)) for pp in ml_pairs]
    ml_k = [col(C_MK + h * 128, 128) for h in range(ML_HEADS)]
    ml_v = [col(C_MV + h * 128, 128) for h in range(ML_HEADS)]
    ml_s2raw = [_dot_nt(ml_q2[pp], _bf(_blockdiag_rows(ml_k[2 * pp], ml_k[2 * pp + 1]))) for pp in ml_pairs]
    ml_qc = [_dot(ml_q2[h // 2][:, (h % 2) * 128:(h % 2 + 1) * 128], _bf(mc_ref[0, 0, h])) for h in range(ML_HEADS)]

    kk = col(C_HK, HG_WIDTH)
    qq = col(C_HQ, HG_WIDTH)
    vv = col(C_HI, HG_WIDTH)
    qq_b, kk_b, vv_b = _bf(qq), _bf(kk), _bf(vv)
    hg_sl = [slice(h * 128, (h + 1) * 128) for h in range(HG_HEADS)]
    hg_adiag = [_dot_nt(qq_b[:, pp * 256:(pp + 1) * 256],
                        _blockdiag_rows(kk_b[:, hg_sl[2 * pp]], kk_b[:, hg_sl[2 * pp + 1]])) for pp in hg_pairs]

    ssd_bm, ssd_cb2, ssd_yi = [], [], []
    for g in range(SSD_GROUPS):
        bm = xbc_a[:, SSD_WIDTH + g * SSD_STATE:SSD_WIDTH + (g + 1) * SSD_STATE]
        cm = xbc_a[:, SSD_WIDTH + (SSD_GROUPS + g) * SSD_STATE:SSD_WIDTH + (SSD_GROUPS + g + 1) * SSD_STATE]
        bm_b, cm_b = _bf(bm), _bf(cm)
        ssd_bm.append(bm)
        ssd_cb2.append(_dot_nt(cm_b, jnp.concatenate([bm_b, bm_b], axis=0)))
        ssd_yi.append(_dot(cm_b, _bf(ht_ref[:, g * 512:(g + 1) * 512])))

    dt_f, ac_f = ex[0:L], ex[L:2 * L]
    eac_f = jnp.exp2(ac_f)
    dend_f = jnp.exp2(ac_f[L - 1:L, :] - ac_f)
    xt = xs * dt_f
    ctr = jnp.sum(jnp.where(dsel_ref[...] > 0.5, ac_f, 0.0), axis=0, keepdims=True)
    xdec = _bf(xt * dend_f)
    eac_end = eac_f[L - 1:L, :]

    qk_lv = []
    for li in range(nl):
        isq = isq_ref[:, li:li + 1] > 0.5
        qk_lv.append(_bf(jnp.where(isq, qq, kk) * jnp.exp2(exps[li * L:(li + 1) * L])))
    gc = exps[nl * L:(nl + 1) * L]
    gend = exps[(nl + 1) * L:(nl + 2) * L]
    q_dec = _bf(qq * jnp.exp2(gc))
    k_dec = _bf(kk * jnp.exp2(gend))
    s_dec = jnp.exp2(gc[L - 1:L, :])

    hg_masks = [hmask_ref[li] > 0.5 for li in range(nl + 1)]
    hg_a2 = []
    for pp in hg_pairs:
        a2 = jnp.where(hg_masks[nl], hg_adiag[pp], 0.0)
        for li in range(nl):
            x = qk_lv[li]
            a_l = _dot_nt(x[:, pp * 256:(pp + 1) * 256], _blockdiag_rows(x[:, hg_sl[2 * pp]], x[:, hg_sl[2 * pp + 1]]))
            a2 = jnp.where(hg_masks[li], a_l, a2)
        hg_a2.append(_bf(a2))
    hg_oi = [_dot_nt(q_dec[:, hg_sl[h]], _bf(hgt_ref[h])) for h in range(HG_HEADS)]

    dsel128 = dsel_ref[:, 0:128] > 0.5
    ml_s2, ml_dens = [], []
    for pp in ml_pairs:
        l0, l1 = LANE_MF + 2 * pp, LANE_MF + 2 * pp + 1
        m_row = jnp.where(lo_half, m_run[:, l0:l0 + 1], m_run[:, l1:l1 + 1])
        u_row = jnp.where(lo_half, u[:, l0:l0 + 1], u[:, l1:l1 + 1])
        u_col = jnp.sum(jnp.where(dsel128, u_row, 0.0), axis=0, keepdims=True)
        s2 = ml_s2raw[pp] * jnp.exp(jnp.where(causal2, u_col - m_row, NEG_INF))
        ml_s2.append(s2)
        ml_dens.append((jnp.sum(jnp.where(lo_half, s2, 0.0), axis=-1, keepdims=True),
                        jnp.sum(jnp.where(lo_half, 0.0, s2), axis=-1, keepdims=True)))
    ml_num2 = [_dot(_bf(ml_s2[pp]), _bf(_blockdiag_rows(ml_v[2 * pp], ml_v[2 * pp + 1]))) for pp in ml_pairs]

    y_parts = []
    for g in range(SSD_GROUPS):
        gs = slice(g * 512, (g + 1) * 512)
        for pp in range(4):
            ps = slice((g * 4 + pp) * 128, (g * 4 + pp + 1) * 128)
            seg = ac_f[:, ps] - ctr[:, ps]
            m2 = _bf(ssd_cb2[g] * jnp.exp2(jnp.where(causal2, seg, NEG_INF)))
            xp = xt[:, ps]
            xb = _bf(jnp.concatenate([jnp.where(lo_half, xp, 0.0), jnp.where(lo_half, 0.0, xp)], axis=0))
            y_parts.append(_dot(m2, xb) + eac_f[:, ps] * ssd_yi[g][:, pp * 128:(pp + 1) * 128])
        ht_ref[:, gs] = ht_ref[:, gs] * eac_end[:, gs] + _dot(_bf(ssd_bm[g].T), xdec[:, gs])

    hg_o2 = [_dot(hg_a2[pp], _blockdiag_rows(vv_b[:, hg_sl[2 * pp]], vv_b[:, hg_sl[2 * pp + 1]]))
             for pp in hg_pairs]

    for h in range(ML_HEADS):
        pp, j, ln = h // 2, h % 2, LANE_MF + h
        qh = col(C_MQ + h * 128, 128)
        wi = w_inter[:, ln:ln + 1]
        num = ml_num2[pp][:, j * 128:(j + 1) * 128] + wi * ml_qc[h]
        qn = jnp.sum(qh * mn_ref[0, 0, h:h + 1, :], axis=-1, keepdims=True)
        den = ml_dens[pp][j] + wi * qn
        hval = num * (1.0 / jnp.maximum(jnp.abs(den), inv_floor[:, ln:ln + 1]))
        hn = _rms(hval, mlg_ref[:, h * 128:(h + 1) * 128])
        put(SSD_WIDTH + h * 128, _bf(col(C_MO + h * 128, 128) * hn))
        kw = ml_k[h] * wk_s[:, ln:ln + 1]
        sc = sc_s[:, ln:ln + 1]
        mc_ref[0, 0, h] = sc * mc_ref[0, 0, h] + _dot(_bf(kw.T), _bf(ml_v[h]))
        mn_ref[0, 0, h:h + 1, :] = sc * mn_ref[0, 0, h:h + 1, :] + jnp.sum(kw, axis=0, keepdims=True)
    mm_ref[0, 0] = jnp.where(ml_lane[0:1], m_i[L - 1:L, :], 0.0)

    y = jnp.concatenate(y_parts, axis=1) + dsk_ref[...] * xs
    y_ssd = _rms(y * col(C_Z, SSD_WIDTH), sg_ref[...])
    put(0, _bf(y_ssd))

    for h in range(HG_HEADS):
        hs = hg_sl[h]
        o = hg_o2[h // 2][:, (h % 2) * 128:(h % 2 + 1) * 128] + hg_oi[h]
        on = _rms(o, hgg_ref[:, hs])
        put(SSD_WIDTH + ML_WIDTH + h * 128, _bf(on * col(C_HG + h * 128, 128)))
        hgt_ref[h] = hgt_ref[h] * s_dec[:, hs] + _dot(_bf(vv[:, hs].T), k_dec[:, hs])


def _mixer_kernel(*refs, t_rows, n_chunks, has_state, has_prev):
    it = iter(refs)
    proj_ref = next(it)
    state0 = [next(it) for _ in range(6)] if has_state else None
    sp_ref, cw_ref, cb_ref, dsk_ref, sg_ref, mlg_ref, hgg_ref = (next(it) for _ in range(7))
    e3_ref, dsel_ref, causal_ref, ce3_ref, hmask_ref, isq_ref = (next(it) for _ in range(6))
    if has_prev:
        for _ in range(6):
            next(it)
    mix_ref, conv_ref, ssd_ref, mc_ref, mn_ref, mm_ref, hg_ref = (next(it) for _ in range(7))
    cbuf_ref, ht_ref, hgt_ref = (next(it) for _ in range(3))
    padded = t_rows < CHUNK
    pbuf_ref = next(it) if padded else None
    tb = pl.program_id(1)

    @pl.when(tb == 0)
    def _init():
        if has_state:
            conv0_ref, ssd0_ref, mc0_ref, mn0_ref, mm0_ref, hg0_ref = state0
            cbuf_ref[5:8, :] = conv0_ref[0, 0]
            ht_ref[...] = ssd0_ref[0, 0].reshape(SSD_WIDTH, SSD_STATE).T
            mc_ref[...] = mc0_ref[...]
            mn_ref[...] = mn0_ref[...]
            mm_ref[...] = mm0_ref[...]
            for h in range(HG_HEADS):
                hgt_ref[h] = hg0_ref[0, 0, h].T
        else:
            cbuf_ref[5:8, :] = jnp.zeros((CONV_W - 1, CONV_CH), jnp.float32)
            ht_ref[...] = jnp.zeros_like(ht_ref)
            mc_ref[...] = jnp.zeros_like(mc_ref)
            mn_ref[...] = jnp.zeros_like(mn_ref)
            mm_ref[...] = jnp.zeros_like(mm_ref)
            hgt_ref[...] = jnp.zeros_like(hgt_ref)

    step = functools.partial(
        _mixer_chunk, sp_ref=sp_ref, cw_ref=cw_ref, cb_ref=cb_ref, dsk_ref=dsk_ref, sg_ref=sg_ref,
        mlg_ref=mlg_ref, hgg_ref=hgg_ref, e3_ref=e3_ref, dsel_ref=dsel_ref, causal_ref=causal_ref,
        ce3_ref=ce3_ref, hmask_ref=hmask_ref, isq_ref=isq_ref, mc_ref=mc_ref, mn_ref=mn_ref, mm_ref=mm_ref,
        ht_ref=ht_ref, hgt_ref=hgt_ref, cbuf_ref=cbuf_ref, t_valid=t_rows if padded else CHUNK)

    if padded:
        pbuf_ref[0:t_rows, :] = proj_ref[0]
        pbuf_ref[t_rows:CHUNK, :] = jnp.zeros((CHUNK - t_rows, PROJ_COLS), jnp.float32)

        def put(c0, val):
            mix_ref[0, :, c0:c0 + val.shape[1]] = val[0:t_rows]

        step(lambda c0, width: pbuf_ref[:, c0:c0 + width], put)
    else:
        def body(ci, carry):
            rows = pl.ds(pl.multiple_of(ci * CHUNK, CHUNK), CHUNK)

            def put(c0, val):
                mix_ref[0, rows, c0:c0 + val.shape[1]] = val

            step(lambda c0, width: proj_ref[0, rows, c0:c0 + width], put)
            return carry
        lax.fori_loop(0, n_chunks, body, 0, unroll=2 if n_chunks % 2 == 0 else 1)

    @pl.when(tb == pl.num_programs(1) - 1)
    def _fin():
        conv_ref[0, 0] = cbuf_ref[5:8, :]
        ssd_ref[0, 0] = ht_ref[...].T.reshape(SSD_HEADS, SSD_HEADDIM, SSD_STATE)
        for h in range(HG_HEADS):
            hg_ref[0, 0, h] = hgt_ref[h].T


def _mixer(proj, state_in, prev, params, consts, layer, depth):
    b, t, _ = proj.shape
    if t < CHUNK:
        tblk = t
    else:
        tblk = min(TIME_BLOCK, t)
        assert t % tblk == 0 and tblk % CHUNK == 0
    lyr4 = lambda i, j: (layer, i, 0, 0)
    lyr5 = lambda i, j: (layer, i, 0, 0, 0)
    c2 = lambda i, j: (0, 0)
    c3 = lambda i, j: (0, 0, 0)
    state_specs = [
        pl.BlockSpec((1, 1, CONV_W - 1, CONV_CH), lyr4),
        pl.BlockSpec((1, 1, SSD_HEADS, SSD_HEADDIM, SSD_STATE), lyr5),
        pl.BlockSpec((1, 1, ML_HEADS, ML_HD, ML_HD), lyr5),
        pl.BlockSpec((1, 1, ML_HEADS, ML_HD), lyr4),
        pl.BlockSpec((1, 1, 1, 128), lyr4),
        pl.BlockSpec((1, 1, HG_HEADS, HG_HD, HG_HD), lyr5),
    ]
    state_shapes = [
        jax.ShapeDtypeStruct((depth, b, CONV_W - 1, CONV_CH), jnp.float32),
        jax.ShapeDtypeStruct((depth, b, SSD_HEADS, SSD_HEADDIM, SSD_STATE), jnp.float32),
        jax.ShapeDtypeStruct((depth, b, ML_HEADS, ML_HD, ML_HD), jnp.float32),
        jax.ShapeDtypeStruct((depth, b, ML_HEADS, ML_HD), jnp.float32),
        jax.ShapeDtypeStruct((depth, b, 1, 128), jnp.float32),
        jax.ShapeDtypeStruct((depth, b, HG_HEADS, HG_HD, HG_HD), jnp.float32),
    ]
    args = [proj]
    in_specs = [pl.BlockSpec((1, tblk, PROJ_COLS), lambda i, j: (i, j, 0))]
    if state_in is not None:
        args += list(state_in)
        in_specs += state_specs
    args += list(params) + list(consts)
    in_specs += [pl.BlockSpec(a.shape, c3 if a.ndim == 3 else c2) for a in list(params) + list(consts)]
    aliases = {}
    if prev is not None:
        for k, a in enumerate(prev):
            aliases[len(args)] = 1 + k
            args.append(a)
            in_specs.append(pl.BlockSpec(memory_space=pl.ANY))
    scratch = [pltpu.VMEM((8 + CHUNK, CONV_CH), jnp.float32),
               pltpu.VMEM((SSD_STATE, SSD_WIDTH), jnp.float32),
               pltpu.VMEM((HG_HEADS, HG_HD, HG_HD), jnp.float32)]
    if t < CHUNK:
        scratch.append(pltpu.VMEM((CHUNK, PROJ_COLS), jnp.float32))
    return pl.pallas_call(
        functools.partial(_mixer_kernel, t_rows=tblk, n_chunks=max(tblk // CHUNK, 1),
                          has_state=state_in is not None, has_prev=prev is not None),
        grid=(b, t // tblk),
        in_specs=in_specs,
        out_specs=[pl.BlockSpec((1, tblk, D_MIX), lambda i, j: (i, j, 0))] + state_specs,
        out_shape=[jax.ShapeDtypeStruct((b, t, D_MIX), jnp.bfloat16)] + state_shapes,
        scratch_shapes=scratch,
        input_output_aliases=aliases,
        compiler_params=pltpu.CompilerParams(
            dimension_semantics=("parallel", "arbitrary"), vmem_limit_bytes=VMEM_LIMIT),
        name="mixer",
    )(*args)


def _regroup_w_in(w):
    o = {}
    acc = 0
    for name, size in (("z", 1024), ("xbc", 1536), ("dt", 16), ("mq", 512), ("mk", 512), ("mv", 512),
                       ("mi", 4), ("mf", 4), ("mo", 512), ("hq", 512), ("hf", 512), ("hi", 512), ("hg", 512)):
        o[name] = w[:, acc:acc + size]
        acc += size
    pad = jnp.zeros((w.shape[0], W_COLS - C_SM - 24), w.dtype)
    cols = [o["z"], o["xbc"], o["mq"], o["mk"], o["mv"], o["mo"], o["hq"], o["hf"], o["hi"], o["hg"],
            o["dt"], o["mi"], o["mf"], pad]
    return jnp.concatenate(cols, axis=1).astype(jnp.bfloat16)


def _small_params(dt_bias, a_log, ml_bi, ml_bf):
    sp = jnp.zeros((8, 128), jnp.float32)
    sp = sp.at[0, LANE_DT:LANE_DT + SSD_HEADS].set(dt_bias)
    sp = sp.at[1, LANE_DT:LANE_DT + SSD_HEADS].set(a_log)
    sp = sp.at[2, LANE_MF:LANE_MF + ML_HEADS].set(ml_bi)
    sp = sp.at[3, LANE_MF:LANE_MF + ML_HEADS].set(ml_bf)
    return sp


def _run_trunk(x, state_in, lp, norm_final, consts, tm_proj, tm_ffn):
    b, t, _ = x.shape
    assert t % CHUNK == 0 or t < CHUNK, "a partial chunk is only supported for single-chunk sequences"
    depth = len(lp)
    x2d = x.reshape(b * t, D_MODEL)
    states = None
    for l in range(depth):
        p = lp[l]
        params = (p["sp"], p["conv_w"], p["conv_b"], p["d_skip"], p["ssd_gain"], p["ml_gain"], p["hg_gain"])
        proj = _in_proj(x2d, p["norm_mix"], p["w_in"], p["hg_lb"], tm_proj, l)
        res = _mixer(proj.reshape(b, t, PROJ_COLS), state_in, states, params, consts, l, depth)
        states = res[1:]
        x2d = _out_ffn(x2d, res[0].reshape(b * t, D_MIX), p["w_out"], p["norm_ffn"], p["w_ffn_in"],
                       p["w_ffn_out"], norm_final, tm_ffn, l == depth - 1)
    conv, ssd, mc, mn, mm_p, hg = states
    return x2d.reshape(b, t, D_MODEL), (conv, ssd, mc, mn, mm_p[:, :, 0, LANE_MF:LANE_MF + ML_HEADS], hg)


def kernel(x_prompt, x_sample, state_conv, state_ssd, state_mlstm_c, state_mlstm_n, state_mlstm_m, state_hgrn,
           norm_mix, w_in, conv_w, conv_b, dt_bias, a_log, d_skip, ssd_gain, ml_bi, ml_bf, ml_gain,
           hg_lb, hg_gain, w_out, norm_ffn, w_ffn_in, w_ffn_out, norm_final):
    depth = w_in.shape[0]
    f32 = jnp.float32
    consts = _mixer_constants()
    lp = []
    for l in range(depth):
        lp.append({
            "norm_mix": norm_mix[l].reshape(1, D_MODEL),
            "w_in": _regroup_w_in(w_in[l]),
            "sp": _small_params(dt_bias[l], a_log[l], ml_bi[l], ml_bf[l]),
            "conv_w": conv_w[l],
            "conv_b": conv_b[l].reshape(1, CONV_CH),
            "d_skip": jnp.repeat(d_skip[l], SSD_HEADDIM).reshape(1, SSD_WIDTH),
            "ssd_gain": ssd_gain[l].reshape(1, SSD_WIDTH),
            "ml_gain": ml_gain[l].reshape(1, ML_WIDTH),
            "hg_lb": hg_lb.astype(f32),
            "hg_gain": hg_gain[l].reshape(1, HG_WIDTH),
            "w_out": w_out[l].astype(jnp.bfloat16),
            "norm_ffn": norm_ffn[l].reshape(1, D_MODEL),
            "w_ffn_in": w_ffn_in[l].astype(jnp.bfloat16),
            "w_ffn_out": w_ffn_out[l].astype(jnp.bfloat16),
        })
    nf = norm_final.reshape(1, D_MODEL)

    y_prompt, p_outs = _run_trunk(x_prompt, None, lp, nf, consts, 256, 512)
    bs = x_sample.shape[0]
    mm_p = jnp.zeros((depth, bs, 1, 128), f32).at[:, :, 0, LANE_MF:LANE_MF + ML_HEADS].set(state_mlstm_m)
    s_in = (state_conv, state_ssd, state_mlstm_c, state_mlstm_n, mm_p, state_hgrn)
    y_sample, s_outs = _run_trunk(x_sample, s_in, lp, nf, consts, 256, 512)
    return (y_prompt, y_sample) + p_outs + s_outs
```

```python
import functools

import numpy as np
import jax
import jax.numpy as jnp
from jax import lax
from jax.experimental import pallas as pl
from jax.experimental.pallas import tpu as pltpu

D_MODEL = 1024
CHUNK = 64
SSD_WIDTH = 1024
SSD_HEADDIM = 64
SSD_HEADS = 16
SSD_GROUPS = 2
SSD_STATE = 128
CONV_W = 4
CONV_CH = SSD_WIDTH + 2 * SSD_GROUPS * SSD_STATE
ML_WIDTH = 512
ML_HEADS = 4
ML_HD = 128
HG_WIDTH = 512
HG_HEADS = 4
HG_HD = 128
D_MIX = 2048
D_FF = 2816
EPS = 1e-6

C_Z = 0
C_XBC = 1024
C_MQ, C_MK, C_MV, C_MO = 2560, 3072, 3584, 4096
C_HQ, C_HK, C_HI, C_HG = 4608, 5120, 5632, 6144
C_SM = 6656
W_COLS = 6784
C_HL = 6784
PROJ_COLS = 7296
TIME_BLOCK = 512
LOG2E = 1.4426950408889634
LANE_DT = 0
LANE_MI = 16
LANE_MF = 20

VMEM_LIMIT = 52 * 1024 * 1024
NEG_INF = float("-inf")

HG_LEVELS = (32, 16, 8, 4, 2, 1)

_NT = (((1,), (1,)), ((), ()))


def _bf(x):
    return x.astype(jnp.bfloat16)


def _dot(a, b):
    return jnp.dot(a, b, preferred_element_type=jnp.float32)


def _dot_nt(a, b):
    return lax.dot_general(a, b, _NT, preferred_element_type=jnp.float32)


def _sigmoid(x):
    return 1.0 / (1.0 + jnp.exp(-x))


def _silu(x):
    return x * _sigmoid(x)


def _softplus(x):
    return jnp.maximum(x, 0.0) + jnp.log(1.0 + jnp.exp(-jnp.abs(x)))


def _split3(x):
    hi = _bf(x)
    r1 = x - hi.astype(jnp.float32)
    mid = _bf(r1)
    r2 = r1 - mid.astype(jnp.float32)
    return hi, mid, _bf(r2)


def _rms(x, gain):
    return x * lax.rsqrt(jnp.mean(x * x, axis=-1, keepdims=True) + EPS) * gain


def _hg_lower_bound(lb_all, layer):
    lb_e = jnp.exp(lb_all - jnp.max(lb_all, axis=0, keepdims=True))
    lb_soft = lb_e * (1.0 / jnp.sum(lb_e, axis=0, keepdims=True))
    return jnp.sum(lb_soft[0:layer + 1], axis=0, keepdims=True) - lb_soft[0:1]


def _causal_conv(cbuf_ref, cw_ref, cb_ref, rows):
    acc = cb_ref[...] + cbuf_ref[8:8 + rows, :] * cw_ref[CONV_W - 1:CONV_W, :]
    for w in range(CONV_W - 1):
        acc = acc + cbuf_ref[5 + w:5 + w + rows, :] * cw_ref[w:w + 1, :]
    return acc


_PROJ_GROUPS = (
    (C_Z, 512, "silu"), (C_Z + 512, 512, "silu"),
    (C_XBC, 512, None), (C_XBC + 512, 512, None), (C_XBC + 1024, 512, None),
    (C_MQ, 512, None), (C_MK, 512, "kscale"), (C_MV, 512, None), (C_MO, 512, "sigmoid"),
    (C_HQ, 512, "silu"), (C_HK, 512, "fgate"), (C_HI, 512, None), (C_HG, 512, "silu"),
    (C_SM, 128, None),
)


def _proj_group(h, w_ref, lb_ref, layer, group, store):
    c0, width, act = group
    acc = _dot(h, w_ref[:, c0:c0 + width])
    if act == "silu":
        acc = _silu(acc)
    elif act == "sigmoid":
        acc = _sigmoid(acc)
    elif act == "kscale":
        acc = acc * (ML_HD ** -0.5)
    elif act == "fgate":
        lb = _hg_lower_bound(lb_ref[...], layer)
        fg = lb + (1.0 - lb) * _sigmoid(acc)
        store(C_HL, jnp.log2(fg))
        acc = 1.0 - fg
    store(c0, acc)


def _in_proj_kernel(x_ref, g_ref, w_ref, lb_ref, o_ref, *, layer):
    h = _bf(_rms(x_ref[...], g_ref[...]))

    def store(c0, val):
        o_ref[:, c0:c0 + val.shape[1]] = val

    for group in _PROJ_GROUPS:
        _proj_group(h, w_ref, lb_ref, layer, group, store)


def _in_proj(x2d, gain, w, hg_lb, tm, layer):
    n = x2d.shape[0]
    c2 = lambda i: (0, 0)
    return pl.pallas_call(
        functools.partial(_in_proj_kernel, layer=layer),
        grid=(n // tm,),
        in_specs=[
            pl.BlockSpec((tm, D_MODEL), lambda i: (i, 0)),
            pl.BlockSpec((1, D_MODEL), c2),
            pl.BlockSpec((None, D_MODEL, W_COLS), lambda i: (layer, 0, 0), pipeline_mode=pl.Buffered(1)),
            pl.BlockSpec(hg_lb.shape, c2),
        ],
        out_specs=pl.BlockSpec((tm, PROJ_COLS), lambda i: (i, 0)),
        out_shape=jax.ShapeDtypeStruct((n, PROJ_COLS), jnp.float32),
        compiler_params=pltpu.CompilerParams(
            dimension_semantics=("parallel",), vmem_limit_bytes=VMEM_LIMIT),
        name="in_proj",
    )(x2d, gain, w, hg_lb)


FF_TILE = 256


def _out_ffn_kernel(x_ref, mix_ref, wo_ref, gn_ref, wi_ref, wf_ref, gf_ref, o_ref, *, final_norm):
    x1 = x_ref[...] + _dot(mix_ref[...], wo_ref[...])
    h2 = _bf(_rms(x1, gn_ref[...]))
    acc = x1
    for j in range(D_FF // FF_TILE):
        g = _dot(h2, wi_ref[:, j * FF_TILE:(j + 1) * FF_TILE])
        u = _dot(h2, wi_ref[:, D_FF + j * FF_TILE:D_FF + (j + 1) * FF_TILE])
        acc = acc + _dot(_bf(_silu(g) * u), wf_ref[j * FF_TILE:(j + 1) * FF_TILE, :])
    if final_norm:
        acc = _rms(acc, gf_ref[...])
    o_ref[...] = acc


def _out_ffn(x2d, mix2d, w_out, g_ffn, w_fi, w_fo, g_final, tm, layer, final_norm):
    n = x2d.shape[0]
    const = lambda i: (0, 0)
    lyr = lambda i: (layer, 0, 0)
    one = pl.Buffered(1)
    return pl.pallas_call(
        functools.partial(_out_ffn_kernel, final_norm=final_norm),
        grid=(n // tm,),
        in_specs=[
            pl.BlockSpec((tm, D_MODEL), lambda i: (i, 0)),
            pl.BlockSpec((tm, D_MIX), lambda i: (i, 0)),
            pl.BlockSpec((None, D_MIX, D_MODEL), lyr, pipeline_mode=one),
            pl.BlockSpec((1, D_MODEL), const),
            pl.BlockSpec((None, D_MODEL, 2 * D_FF), lyr, pipeline_mode=one),
            pl.BlockSpec((None, D_FF, D_MODEL), lyr, pipeline_mode=one),
            pl.BlockSpec((1, D_MODEL), const),
        ],
        out_specs=pl.BlockSpec((tm, D_MODEL), lambda i: (i, 0)),
        out_shape=jax.ShapeDtypeStruct((n, D_MODEL), jnp.float32),
        compiler_params=pltpu.CompilerParams(
            dimension_semantics=("parallel",), vmem_limit_bytes=VMEM_LIMIT),
        name="out_ffn",
    )(x2d, mix2d, w_out, g_ffn, w_fi, w_fo, g_final)


def _mixer_constants():
    L = CHUNK
    e3 = np.zeros((128, SSD_WIDTH), np.float32)
    for piece in range(3):
        for h in range(SSD_HEADS):
            e3[piece * SSD_HEADS + h, h * SSD_HEADDIM:(h + 1) * SSD_HEADDIM] = 1.0

    l = np.arange(L)[:, None]
    c = np.arange(SSD_WIDTH)[None, :]
    dsel = ((c % L) == l).astype(np.float32)
    s2 = np.arange(128)[None, :] % L
    causal2 = (s2 <= l).astype(np.float32)

    t = np.arange(L)
    blocks = []
    masks = []
    isq = np.zeros((L, 128), np.float32)
    for li, m in enumerate(HG_LEVELS):
        cm = np.zeros((L, L), np.float32)
        start = (t // (2 * m)) * (2 * m)
        mid = start + m - 1
        query = (t - start) >= m
        for s in range(L):
            if query[s]:
                cm[s, mid[s] + 1:s + 1] = 1.0
            else:
                cm[s, s + 1:mid[s] + 1] = 1.0
        blocks.append(cm)
        same = (start[:, None] == start[None, :])
        mk = same & query[:, None] & (~query[None, :])
        masks.append(np.concatenate([mk, mk], axis=1).astype(np.float32))
        isq[:, li] = query.astype(np.float32)
    tri = (t[None, :] <= t[:, None]).astype(np.float32)
    blocks.append(tri)
    blocks.append((t[None, :] > t[:, None]).astype(np.float32))
    ce = np.concatenate(blocks, axis=0)
    ce3 = np.concatenate([ce, ce, ce], axis=1)
    diag = (t[:, None] == t[None, :])
    masks.append(np.concatenate([diag, diag], axis=1).astype(np.float32))
    hmask = np.stack(masks, axis=0)
    return (jnp.asarray(e3, jnp.bfloat16), jnp.asarray(dsel), jnp.asarray(causal2),
            jnp.asarray(ce3, jnp.bfloat16), jnp.asarray(hmask), jnp.asarray(isq))


def _scan_time(x, op, identity, row):
    sh = 1
    while sh < x.shape[0]:
        r = pltpu.roll(x, sh, axis=0)
        x = op(x, jnp.where(row >= sh, r, identity))
        sh *= 2
    return x


def _blockdiag_rows(a, b):
    z = jnp.zeros_like(a)
    return jnp.concatenate([jnp.concatenate([a, z], axis=1), jnp.concatenate([z, b], axis=1)], axis=0)


def _mixer_chunk(col, put, sp_ref, cw_ref, cb_ref, dsk_ref, sg_ref, mlg_ref, hgg_ref,
                 e3_ref, dsel_ref, causal_ref, ce3_ref, hmask_ref, isq_ref,
                 mc_ref, mn_ref, mm_ref, ht_ref, hgt_ref, cbuf_ref, *, t_valid):
    L = CHUNK
    row = lax.broadcasted_iota(jnp.int32, (L, 128), 0)
    lane = lax.broadcasted_iota(jnp.int32, (L, 128), 1)
    lo_half = lane < 64
    causal2 = causal_ref[...] > 0.5

    nl = len(HG_LEVELS)
    ml_pairs = range(ML_HEADS // 2)
    hg_pairs = range(HG_HEADS // 2)

    sm = col(C_SM, 128)
    dt = _softplus(sm + sp_ref[0:1, :])
    a_neg2 = jnp.where(lane[0:1] < SSD_HEADS, -LOG2E * jnp.exp(sp_ref[1:2, :]), 0.0)
    ig = pltpu.roll(sm, LANE_MF - LANE_MI, axis=1) + sp_ref[2:3, :]
    lf = -_softplus(-(sm + sp_ref[3:4, :]))
    if t_valid < L:
        ok = row < t_valid
        dt = jnp.where(ok, dt, 0.0)
        ig = jnp.where(ok, ig, NEG_INF)
        lf = jnp.where(ok, lf, 0.0)
    ml_lane = (lane >= LANE_MF) & (lane < LANE_MF + ML_HEADS)
    cs = _scan_time(jnp.where(lane < SSD_HEADS, dt * a_neg2, jnp.where(ml_lane, lf, 0.0)), jnp.add, 0.0, row)
    acum2 = cs
    u = jnp.where(ml_lane, ig - cs, 0.0)
    m_prev = mm_ref[0, 0]
    m_run = jnp.maximum(_scan_time(u, jnp.maximum, NEG_INF, row), m_prev)
    m_i = cs + m_run
    m_last = m_run[L - 1:L, :]
    w_inter = jnp.exp(m_prev - m_run)
    inv_floor = jnp.exp(-m_i)
    wk_s = jnp.exp(u - m_last)
    sc_s = jnp.exp(m_prev - m_last)

    hi, mid, lo = _split3(jnp.concatenate([dt, acum2], axis=0))
    lane2 = lax.broadcasted_iota(jnp.int32, (2 * L, 128), 1)
    packed = jnp.where(
        lane2 < SSD_HEADS, hi.astype(jnp.float32),
        jnp.where(lane2 < 2 * SSD_HEADS, pltpu.roll(mid.astype(jnp.float32), SSD_HEADS, axis=1),
                  jnp.where(lane2 < 3 * SSD_HEADS, pltpu.roll(lo.astype(jnp.float32), 2 * SSD_HEADS, axis=1), 0.0)))
    ex = _dot(_bf(packed), e3_ref[...])
    glog2 = col(C_HL, HG_WIDTH)
    if t_valid < L:
        glog2 = jnp.where(lax.broadcasted_iota(jnp.int32, (L, HG_WIDTH), 0) < t_valid, glog2, 0.0)
    ghi, gmid, glo = _split3(glog2)
    exps = _dot(ce3_ref[...], jnp.concatenate([ghi, gmid, glo], axis=0))

    xbc = col(C_XBC, CONV_CH)
    cbuf_ref[8:8 + L, :] = xbc
    xbc_a = _silu(_causal_conv(cbuf_ref, cw_ref, cb_ref, L))
    cbuf_ref[5:8, :] = xbc[t_valid - 3:t_valid, :]
    xs = xbc_a[:, 0:SSD_WIDTH]

    ml_q2 = [_bf(col(C_MQ + pp * 256, 256)) for pp in ml_pairs]
    ml_k = [col(C_MK + h * 128, 128) for h in range(ML_HEADS)]
    ml_v = [col(C_MV + h * 128, 128) for h in range(ML_HEADS)]
    ml_s2raw = [_dot_nt(ml_q2[pp], _bf(_blockdiag_rows(ml_k[2 * pp], ml_k[2 * pp + 1]))) for pp in ml_pairs]
    ml_qc = [_dot(ml_q2[h // 2][:, (h % 2) * 128:(h % 2 + 1) * 128], _bf(mc_ref[0, 0, h])) for h in range(ML_HEADS)]

    kk = col(C_HK, HG_WIDTH)
    qq = col(C_HQ, HG_WIDTH)
    vv = col(C_HI, HG_WIDTH)
    qq_b, kk_b, vv_b = _bf(qq), _bf(kk), _bf(vv)
    hg_sl = [slice(h * 128, (h + 1) * 128) for h in range(HG_HEADS)]
    hg_adiag = [_dot_nt(qq_b[:, pp * 256:(pp + 1) * 256],
                        _blockdiag_rows(kk_b[:, hg_sl[2 * pp]], kk_b[:, hg_sl[2 * pp + 1]])) for pp in hg_pairs]

    ssd_bm, ssd_cb2, ssd_yi = [], [], []
    for g in range(SSD_GROUPS):
        bm = xbc_a[:, SSD_WIDTH + g * SSD_STATE:SSD_WIDTH + (g + 1) * SSD_STATE]
        cm = xbc_a[:, SSD_WIDTH + (SSD_GROUPS + g) * SSD_STATE:SSD_WIDTH + (SSD_GROUPS + g + 1) * SSD_STATE]
        bm_b, cm_b = _bf(bm), _bf(cm)
        ssd_bm.append(bm)
        ssd_cb2.append(_dot_nt(cm_b, jnp.concatenate([bm_b, bm_b], axis=0)))
        ssd_yi.append(_dot(cm_b, _bf(ht_ref[:, g * 512:(g + 1) * 512])))

    dt_f, ac_f = ex[0:L], ex[L:2 * L]
    eac_f = jnp.exp2(ac_f)
    dend_f = jnp.exp2(ac_f[L - 1:L, :] - ac_f)
    xt = xs * dt_f
    ctr = jnp.sum(jnp.where(dsel_ref[...] > 0.5, ac_f, 0.0), axis=0, keepdims=True)
    xdec = _bf(xt * dend_f)
    eac_end = eac_f[L - 1:L, :]

    qk_lv = []
    for li in range(nl):
        isq = isq_ref[:, li:li + 1] > 0.5
        qk_lv.append(_bf(jnp.where(isq, qq, kk) * jnp.exp2(exps[li * L:(li + 1) * L])))
    gc = exps[nl * L:(nl + 1) * L]
    gend = exps[(nl + 1) * L:(nl + 2) * L]
    q_dec = _bf(qq * jnp.exp2(gc))
    k_dec = _bf(kk * jnp.exp2(gend))
    s_dec = jnp.exp2(gc[L - 1:L, :])

    hg_masks = [hmask_ref[li] > 0.5 for li in range(nl + 1)]
    hg_a2 = []
    for pp in hg_pairs:
        a2 = jnp.where(hg_masks[nl], hg_adiag[pp], 0.0)
        for li in range(nl):
            x = qk_lv[li]
            a_l = _dot_nt(x[:, pp * 256:(pp + 1) * 256], _blockdiag_rows(x[:, hg_sl[2 * pp]], x[:, hg_sl[2 * pp + 1]]))
            a2 = jnp.where(hg_masks[li], a_l, a2)
        hg_a2.append(_bf(a2))
    hg_oi = [_dot_nt(q_dec[:, hg_sl[h]], _bf(hgt_ref[h])) for h in range(HG_HEADS)]

    dsel128 = dsel_ref[:, 0:128] > 0.5
    ml_s2, ml_dens = [], []
    for pp in ml_pairs:
        l0, l1 = LANE_MF + 2 * pp, LANE_MF + 2 * pp + 1
        m_row = jnp.where(lo_half, m_run[:, l0:l0 + 1], m_run[:, l1:l1 + 1])
        u_row = jnp.where(lo_half, u[:, l0:l0 + 1], u[:, l1:l1 + 1])
        u_col = jnp.sum(jnp.where(dsel128, u_row, 0.0), axis=0, keepdims=True)
        s2 = ml_s2raw[pp] * jnp.exp(jnp.where(causal2, u_col - m_row, NEG_INF))
        ml_s2.append(s2)
        ml_dens.append((jnp.sum(jnp.where(lo_half, s2, 0.0), axis=-1, keepdims=True),
                        jnp.sum(jnp.where(lo_half, 0.0, s2), axis=-1, keepdims=True)))
    ml_num2 = [_dot(_bf(ml_s2[pp]), _bf(_blockdiag_rows(ml_v[2 * pp], ml_v[2 * pp + 1]))) for pp in ml_pairs]

    y_parts = []
    for g in range(SSD_GROUPS):
        gs = slice(g * 512, (g + 1) * 512)
        for pp in range(4):
            ps = slice((g * 4 + pp) * 128, (g * 4 + pp + 1) * 128)
            seg = ac_f[:, ps] - ctr[:, ps]
            m2 = _bf(ssd_cb2[g] * jnp.exp2(jnp.where(causal2, seg, NEG_INF)))
            xp = xt[:, ps]
            xb = _bf(jnp.concatenate([jnp.where(lo_half, xp, 0.0), jnp.where(lo_half, 0.0, xp)], axis=0))
            y_parts.append(_dot(m2, xb) + eac_f[:, ps] * ssd_yi[g][:, pp * 128:(pp + 1) * 128])
        ht_ref[:, gs] = ht_ref[:, gs] * eac_end[:, gs] + _dot(_bf(ssd_bm[g].T), xdec[:, gs])

    hg_o2 = [_dot(hg_a2[pp], _blockdiag_rows(vv_b[:, hg_sl[2 * pp]], vv_b[:, hg_sl[2 * pp + 1]]))
             for pp in hg_pairs]

    for h in range(ML_HEADS):
        pp, j, ln = h // 2, h % 2, LANE_MF + h
        qh = col(C_MQ + h * 128, 128)
        wi = w_inter[:, ln:ln + 1]
        num = ml_num2[pp][:, j * 128:(j + 1) * 128] + wi * ml_qc[h]
        qn = jnp.sum(qh * mn_ref[0, 0, h:h + 1, :], axis=-1, keepdims=True)
        den = ml_dens[pp][j] + wi * qn
        hval = num * (1.0 / jnp.maximum(jnp.abs(den), inv_floor[:, ln:ln + 1]))
        hn = _rms(hval, mlg_ref[:, h * 128:(h + 1) * 128])
        put(SSD_WIDTH + h * 128, _bf(col(C_MO + h * 128, 128) * hn))
        kw = ml_k[h] * wk_s[:, ln:ln + 1]
        sc = sc_s[:, ln:ln + 1]
        mc_ref[0, 0, h] = sc * mc_ref[0, 0, h] + _dot(_bf(kw.T), _bf(ml_v[h]))
        mn_ref[0, 0, h:h + 1, :] = sc * mn_ref[0, 0, h:h + 1, :] + jnp.sum(kw, axis=0, keepdims=True)
    mm_ref[0, 0] = jnp.where(ml_lane[0:1], m_i[L - 1:L, :], 0.0)

    y = jnp.concatenate(y_parts, axis=1) + dsk_ref[...] * xs
    y_ssd = _rms(y * col(C_Z, SSD_WIDTH), sg_ref[...])
    put(0, _bf(y_ssd))

    for h in range(HG_HEADS):
        hs = hg_sl[h]
        o = hg_o2[h // 2][:, (h % 2) * 128:(h % 2 + 1) * 128] + hg_oi[h]
        on = _rms(o, hgg_ref[:, hs])
        put(SSD_WIDTH + ML_WIDTH + h * 128, _bf(on * col(C_HG + h * 128, 128)))
        hgt_ref[h] = hgt_ref[h] * s_dec[:, hs] + _dot(_bf(vv[:, hs].T), k_dec[:, hs])


def _mixer_kernel(*refs, t_rows, n_chunks, has_state, has_prev):
    it = iter(refs)
    proj_ref = next(it)
    state0 = [next(it) for _ in range(6)] if has_state else None
    sp_ref, cw_ref, cb_ref, dsk_ref, sg_ref, mlg_ref, hgg_ref = (next(it) for _ in range(7))
    e3_ref, dsel_ref, causal_ref, ce3_ref, hmask_ref, isq_ref = (next(it) for _ in range(6))
    if has_prev:
        for _ in range(6):
            next(it)
    mix_ref, conv_ref, ssd_ref, mc_ref, mn_ref, mm_ref, hg_ref = (next(it) for _ in range(7))
    cbuf_ref, ht_ref, hgt_ref = (next(it) for _ in range(3))
    padded = t_rows < CHUNK
    pbuf_ref = next(it) if padded else None
    tb = pl.program_id(1)

    @pl.when(tb == 0)
    def _init():
        if has_state:
            conv0_ref, ssd0_ref, mc0_ref, mn0_ref, mm0_ref, hg0_ref = state0
            cbuf_ref[5:8, :] = conv0_ref[0, 0]
            ht_ref[...] = ssd0_ref[0, 0].reshape(SSD_WIDTH, SSD_STATE).T
            mc_ref[...] = mc0_ref[...]
            mn_ref[...] = mn0_ref[...]
            mm_ref[...] = mm0_ref[...]
            for h in range(HG_HEADS):
                hgt_ref[h] = hg0_ref[0, 0, h].T
        else:
            cbuf_ref[5:8, :] = jnp.zeros((CONV_W - 1, CONV_CH), jnp.float32)
            ht_ref[...] = jnp.zeros_like(ht_ref)
            mc_ref[...] = jnp.zeros_like(mc_ref)
            mn_ref[...] = jnp.zeros_like(mn_ref)
            mm_ref[...] = jnp.zeros_like(mm_ref)
            hgt_ref[...] = jnp.zeros_like(hgt_ref)

    step = functools.partial(
        _mixer_chunk, sp_ref=sp_ref, cw_ref=cw_ref, cb_ref=cb_ref, dsk_ref=dsk_ref, sg_ref=sg_ref,
        mlg_ref=mlg_ref, hgg_ref=hgg_ref, e3_ref=e3_ref, dsel_ref=dsel_ref, causal_ref=causal_ref,
        ce3_ref=ce3_ref, hmask_ref=hmask_ref, isq_ref=isq_ref, mc_ref=mc_ref, mn_ref=mn_ref, mm_ref=mm_ref,
        ht_ref=ht_ref, hgt_ref=hgt_ref, cbuf_ref=cbuf_ref, t_valid=t_rows if padded else CHUNK)

    if padded:
        pbuf_ref[0:t_rows, :] = proj_ref[0]
        pbuf_ref[t_rows:CHUNK, :] = jnp.zeros((CHUNK - t_rows, PROJ_COLS), jnp.float32)

        def put(c0, val):
            mix_ref[0, :, c0:c0 + val.shape[1]] = val[0:t_rows]

        step(lambda c0, width: pbuf_ref[:, c0:c0 + width], put)
    else:
        def body(ci, carry):
            rows = pl.ds(pl.multiple_of(ci * CHUNK, CHUNK), CHUNK)

            def put(c0, val):
                mix_ref[0, rows, c0:c0 + val.shape[1]] = val

            step(lambda c0, width: proj_ref[0, rows, c0:c0 + width], put)
            return carry
        lax.fori_loop(0, n_chunks, body, 0, unroll=2 if n_chunks % 2 == 0 else 1)

    @pl.when(tb == pl.num_programs(1) - 1)
    def _fin():
        conv_ref[0, 0] = cbuf_ref[5:8, :]
        ssd_ref[0, 0] = ht_ref[...].T.reshape(SSD_HEADS, SSD_HEADDIM, SSD_STATE)
        for h in range(HG_HEADS):
            hg_ref[0, 0, h] = hgt_ref[h].T


def _mixer(proj, state_in, prev, params, consts, layer, depth):
    b, t, _ = proj.shape
    if t < CHUNK:
        tblk = t
    else:
        tblk = min(TIME_BLOCK, t)
        assert t % tblk == 0 and tblk % CHUNK == 0
    lyr4 = lambda i, j: (layer, i, 0, 0)
    lyr5 = lambda i, j: (layer, i, 0, 0, 0)
    c2 = lambda i, j: (0, 0)
    c3 = lambda i, j: (0, 0, 0)
    state_specs = [
        pl.BlockSpec((1, 1, CONV_W - 1, CONV_CH), lyr4),
        pl.BlockSpec((1, 1, SSD_HEADS, SSD_HEADDIM, SSD_STATE), lyr5),
        pl.BlockSpec((1, 1, ML_HEADS, ML_HD, ML_HD), lyr5),
        pl.BlockSpec((1, 1, ML_HEADS, ML_HD), lyr4),
        pl.BlockSpec((1, 1, 1, 128), lyr4),
        pl.BlockSpec((1, 1, HG_HEADS, HG_HD, HG_HD), lyr5),
    ]
    state_shapes = [
        jax.ShapeDtypeStruct((depth, b, CONV_W - 1, CONV_CH), jnp.float32),
        jax.ShapeDtypeStruct((depth, b, SSD_HEADS, SSD_HEADDIM, SSD_STATE), jnp.float32),
        jax.ShapeDtypeStruct((depth, b, ML_HEADS, ML_HD, ML_HD), jnp.float32),
        jax.ShapeDtypeStruct((depth, b, ML_HEADS, ML_HD), jnp.float32),
        jax.ShapeDtypeStruct((depth, b, 1, 128), jnp.float32),
        jax.ShapeDtypeStruct((depth, b, HG_HEADS, HG_HD, HG_HD), jnp.float32),
    ]
    args = [proj]
    in_specs = [pl.BlockSpec((1, tblk, PROJ_COLS), lambda i, j: (i, j, 0))]
    if state_in is not None:
        args += list(state_in)
        in_specs += state_specs
    args += list(params) + list(consts)
    in_specs += [pl.BlockSpec(a.shape, c3 if a.ndim == 3 else c2) for a in list(params) + list(consts)]
    aliases = {}
    if prev is not None:
        for k, a in enumerate(prev):
            aliases[len(args)] = 1 + k
            args.append(a)
            in_specs.append(pl.BlockSpec(memory_space=pl.ANY))
    scratch = [pltpu.VMEM((8 + CHUNK, CONV_CH), jnp.float32),
               pltpu.VMEM((SSD_STATE, SSD_WIDTH), jnp.float32),
               pltpu.VMEM((HG_HEADS, HG_HD, HG_HD), jnp.float32)]
    if t < CHUNK:
        scratch.append(pltpu.VMEM((CHUNK, PROJ_COLS), jnp.float32))
    return pl.pallas_call(
        functools.partial(_mixer_kernel, t_rows=tblk, n_chunks=max(tblk // CHUNK, 1),
                          has_state=state_in is not None, has_prev=prev is not None),
        grid=(b, t // tblk),
        in_specs=in_specs,
        out_specs=[pl.BlockSpec((1, tblk, D_MIX), lambda i, j: (i, j, 0))] + state_specs,
        out_shape=[jax.ShapeDtypeStruct((b, t, D_MIX), jnp.bfloat16)] + state_shapes,
        scratch_shapes=scratch,
        input_output_aliases=aliases,
        compiler_params=pltpu.CompilerParams(
            dimension_semantics=("parallel", "arbitrary"), vmem_limit_bytes=VMEM_LIMIT),
        name="mixer",
    )(*args)


def _regroup_w_in(w):
    w = w.astype(jnp.bfloat16)
    o = {}
    acc = 0
    for name, size in (("z", 1024), ("xbc", 1536), ("dt", 16), ("mq", 512), ("mk", 512), ("mv", 512),
                       ("mi", 4), ("mf", 4), ("mo", 512), ("hq", 512), ("hf", 512), ("hi", 512), ("hg", 512)):
        o[name] = w[..., acc:acc + size]
        acc += size
    pad = jnp.zeros(w.shape[:-1] + (W_COLS - C_SM - 24,), w.dtype)
    cols = [o["z"], o["xbc"], o["mq"], o["mk"], o["mv"], o["mo"], o["hq"], o["hf"], o["hi"], o["hg"],
            o["dt"], o["mi"], o["mf"], pad]
    return jnp.concatenate(cols, axis=-1)


def _small_params(dt_bias, a_log, ml_bi, ml_bf):
    sp = jnp.zeros((8, 128), jnp.float32)
    sp = sp.at[0, LANE_DT:LANE_DT + SSD_HEADS].set(dt_bias)
    sp = sp.at[1, LANE_DT:LANE_DT + SSD_HEADS].set(a_log)
    sp = sp.at[2, LANE_MF:LANE_MF + ML_HEADS].set(ml_bi)
    sp = sp.at[3, LANE_MF:LANE_MF + ML_HEADS].set(ml_bf)
    return sp


def _run_trunk(x, state_in, lp, norm_final, consts, tm_proj, tm_ffn):
    b, t, _ = x.shape
    assert t % CHUNK == 0 or t < CHUNK, "a partial chunk is only supported for single-chunk sequences"
    depth = len(lp)
    x2d = x.reshape(b * t, D_MODEL)
    states = None
    for l in range(depth):
        p = lp[l]
        params = (p["sp"], p["conv_w"], p["conv_b"], p["d_skip"], p["ssd_gain"], p["ml_gain"], p["hg_gain"])
        proj = _in_proj(x2d, p["norm_mix"], p["w_in"], p["hg_lb"], tm_proj, l)
        res = _mixer(proj.reshape(b, t, PROJ_COLS), state_in, states, params, consts, l, depth)
        states = res[1:]
        x2d = _out_ffn(x2d, res[0].reshape(b * t, D_MIX), p["w_out"], p["norm_ffn"], p["w_ffn_in"],
                       p["w_ffn_out"], norm_final, tm_ffn, l, l == depth - 1)
    conv, ssd, mc, mn, mm_p, hg = states
    return x2d.reshape(b, t, D_MODEL), (conv, ssd, mc, mn, mm_p[:, :, 0, LANE_MF:LANE_MF + ML_HEADS], hg)


def kernel(x_prompt, x_sample, state_conv, state_ssd, state_mlstm_c, state_mlstm_n, state_mlstm_m, state_hgrn,
           norm_mix, w_in, conv_w, conv_b, dt_bias, a_log, d_skip, ssd_gain, ml_bi, ml_bf, ml_gain,
           hg_lb, hg_gain, w_out, norm_ffn, w_ffn_in, w_ffn_out, norm_final):
    depth = w_in.shape[0]
    f32 = jnp.float32
    consts = _mixer_constants()
    bf16 = jnp.bfloat16
    w_in_b, w_out_b, w_fi_b, w_fo_b = _regroup_w_in(w_in), w_out.astype(bf16), w_ffn_in.astype(bf16), w_ffn_out.astype(bf16)
    lp = []
    for l in range(depth):
        lp.append({
            "norm_mix": norm_mix[l].reshape(1, D_MODEL),
            "w_in": w_in_b,
            "sp": _small_params(dt_bias[l], a_log[l], ml_bi[l], ml_bf[l]),
            "conv_w": conv_w[l],
            "conv_b": conv_b[l].reshape(1, CONV_CH),
            "d_skip": jnp.repeat(d_skip[l], SSD_HEADDIM).reshape(1, SSD_WIDTH),
            "ssd_gain": ssd_gain[l].reshape(1, SSD_WIDTH),
            "ml_gain": ml_gain[l].reshape(1, ML_WIDTH),
            "hg_lb": hg_lb.astype(f32),
            "hg_gain": hg_gain[l].reshape(1, HG_WIDTH),
            "w_out": w_out_b,
            "norm_ffn": norm_ffn[l].reshape(1, D_MODEL),
            "w_ffn_in": w_fi_b,
            "w_ffn_out": w_fo_b,
        })
    nf = norm_final.reshape(1, D_MODEL)

    y_prompt, p_outs = _run_trunk(x_prompt, None, lp, nf, consts, 256, 512)
    bs = x_sample.shape[0]
    mm_p = jnp.zeros((depth, bs, 1, 128), f32).at[:, :, 0, LANE_MF:LANE_MF + ML_HEADS].set(state_mlstm_m)
    s_in = (state_conv, state_ssd, state_mlstm_c, state_mlstm_n, mm_p, state_hgrn)
    y_sample, s_outs = _run_trunk(x_sample, s_in, lp, nf, consts, 256, 512)
    return (y_prompt, y_sample) + p_outs + s_outs
```

```python
import functools

import numpy as np
import jax
import jax.numpy as jnp
from jax import lax
from jax.experimental import pallas as pl
from jax.experimental.pallas import tpu as pltpu

D_MODEL = 1024
CHUNK = 64
SSD_WIDTH = 1024
SSD_HEADDIM = 64
SSD_HEADS = 16
SSD_GROUPS = 2
SSD_STATE = 128
CONV_W = 4
CONV_CH = SSD_WIDTH + 2 * SSD_GROUPS * SSD_STATE
ML_WIDTH = 512
ML_HEADS = 4
ML_HD = 128
HG_WIDTH = 512
HG_HEADS = 4
HG_HD = 128
D_MIX = 2048
D_FF = 2816
EPS = 1e-6

C_Z = 0
C_XBC = 1024
C_MQ, C_MK, C_MV, C_MO = 2560, 3072, 3584, 4096
C_HQ, C_HK, C_HI, C_HG = 4608, 5120, 5632, 6144
C_SM = 6656
W_COLS = 6784
C_HL = 6784
PROJ_COLS = 7296
TIME_BLOCK = 256
SEQ_PER_STEP = 1
LOG2E = 1.4426950408889634
LANE_DT = 0
LANE_MI = 16
LANE_MF = 20

VMEM_LIMIT = 52 * 1024 * 1024
NEG_INF = float("-inf")

HG_LEVELS = (32, 16, 8, 4, 2, 1)

_NT = (((1,), (1,)), ((), ()))


def _bf(x):
    return x.astype(jnp.bfloat16)


def _dot(a, b):
    return jnp.dot(a, b, preferred_element_type=jnp.float32)


def _dot_nt(a, b):
    return lax.dot_general(a, b, _NT, preferred_element_type=jnp.float32)


def _sigmoid(x):
    return 1.0 / (1.0 + jnp.exp(-x))


def _silu(x):
    return x * _sigmoid(x)


def _softplus(x):
    return jnp.maximum(x, 0.0) + jnp.log(1.0 + jnp.exp(-jnp.abs(x)))


def _split3(x):
    hi = _bf(x)
    r1 = x - hi.astype(jnp.float32)
    mid = _bf(r1)
    r2 = r1 - mid.astype(jnp.float32)
    return hi, mid, _bf(r2)


def _rms(x, gain):
    return x * lax.rsqrt(jnp.mean(x * x, axis=-1, keepdims=True) + EPS) * gain


def _hg_lower_bound(lb_all, layer):
    lb_e = jnp.exp(lb_all - jnp.max(lb_all, axis=0, keepdims=True))
    lb_soft = lb_e * (1.0 / jnp.sum(lb_e, axis=0, keepdims=True))
    return jnp.sum(lb_soft[0:layer + 1], axis=0, keepdims=True) - lb_soft[0:1]


def _causal_conv(cbuf_ref, cw_ref, cb_ref, rows):
    acc = cb_ref[...] + cbuf_ref[8:8 + rows, :] * cw_ref[CONV_W - 1:CONV_W, :]
    for w in range(CONV_W - 1):
        acc = acc + cbuf_ref[5 + w:5 + w + rows, :] * cw_ref[w:w + 1, :]
    return acc


_PROJ_GROUPS = (
    (C_Z, 512, "silu"), (C_Z + 512, 512, "silu"),
    (C_XBC, 512, None), (C_XBC + 512, 512, None), (C_XBC + 1024, 512, None),
    (C_MQ, 512, None), (C_MK, 512, "kscale"), (C_MV, 512, None), (C_MO, 512, "sigmoid"),
    (C_HQ, 512, "silu"), (C_HK, 512, "fgate"), (C_HI, 512, None), (C_HG, 512, "silu"),
    (C_SM, 128, None),
)


def _proj_group(h, w_ref, lb_ref, layer, group, store):
    c0, width, act = group
    acc = _dot(h, w_ref[:, c0:c0 + width])
    if act == "silu":
        acc = _silu(acc)
    elif act == "sigmoid":
        acc = _sigmoid(acc)
    elif act == "kscale":
        acc = acc * (ML_HD ** -0.5)
    elif act == "fgate":
        lb = _hg_lower_bound(lb_ref[...], layer)
        fg = lb + (1.0 - lb) * _sigmoid(acc)
        store(C_HL, jnp.log2(fg))
        acc = 1.0 - fg
    store(c0, acc)


def _in_proj_kernel(x_ref, g_ref, w_ref, lb_ref, o_ref, *, layer):
    h = _bf(_rms(x_ref[...], g_ref[...]))

    def store(c0, val):
        o_ref[:, c0:c0 + val.shape[1]] = val

    for group in _PROJ_GROUPS:
        _proj_group(h, w_ref, lb_ref, layer, group, store)


def _in_proj(x2d, gain, w, hg_lb, tm, layer):
    n = x2d.shape[0]
    c2 = lambda i: (0, 0)
    return pl.pallas_call(
        functools.partial(_in_proj_kernel, layer=layer),
        grid=(n // tm,),
        in_specs=[
            pl.BlockSpec((tm, D_MODEL), lambda i: (i, 0)),
            pl.BlockSpec((1, D_MODEL), c2),
            pl.BlockSpec((None, D_MODEL, W_COLS), lambda i: (layer, 0, 0), pipeline_mode=pl.Buffered(1)),
            pl.BlockSpec(hg_lb.shape, c2),
        ],
        out_specs=pl.BlockSpec((tm, PROJ_COLS), lambda i: (i, 0)),
        out_shape=jax.ShapeDtypeStruct((n, PROJ_COLS), jnp.float32),
        compiler_params=pltpu.CompilerParams(
            dimension_semantics=("parallel",), vmem_limit_bytes=VMEM_LIMIT),
        name="in_proj",
    )(x2d, gain, w, hg_lb)


FF_TILE = 256


def _out_ffn_kernel(x_ref, mix_ref, wo_ref, gn_ref, wi_ref, wf_ref, gf_ref, o_ref, *, final_norm):
    x1 = x_ref[...] + _dot(mix_ref[...], wo_ref[...])
    h2 = _bf(_rms(x1, gn_ref[...]))
    acc = x1
    for j in range(D_FF // FF_TILE):
        g = _dot(h2, wi_ref[:, j * FF_TILE:(j + 1) * FF_TILE])
        u = _dot(h2, wi_ref[:, D_FF + j * FF_TILE:D_FF + (j + 1) * FF_TILE])
        acc = acc + _dot(_bf(_silu(g) * u), wf_ref[j * FF_TILE:(j + 1) * FF_TILE, :])
    if final_norm:
        acc = _rms(acc, gf_ref[...])
    o_ref[...] = acc


def _out_ffn(x2d, mix2d, w_out, g_ffn, w_fi, w_fo, g_final, tm, layer, final_norm):
    n = x2d.shape[0]
    const = lambda i: (0, 0)
    lyr = lambda i: (layer, 0, 0)
    one = pl.Buffered(1)
    return pl.pallas_call(
        functools.partial(_out_ffn_kernel, final_norm=final_norm),
        grid=(n // tm,),
        in_specs=[
            pl.BlockSpec((tm, D_MODEL), lambda i: (i, 0)),
            pl.BlockSpec((tm, D_MIX), lambda i: (i, 0)),
            pl.BlockSpec((None, D_MIX, D_MODEL), lyr, pipeline_mode=one),
            pl.BlockSpec((1, D_MODEL), const),
            pl.BlockSpec((None, D_MODEL, 2 * D_FF), lyr, pipeline_mode=one),
            pl.BlockSpec((None, D_FF, D_MODEL), lyr, pipeline_mode=one),
            pl.BlockSpec((1, D_MODEL), const),
        ],
        out_specs=pl.BlockSpec((tm, D_MODEL), lambda i: (i, 0)),
        out_shape=jax.ShapeDtypeStruct((n, D_MODEL), jnp.float32),
        compiler_params=pltpu.CompilerParams(
            dimension_semantics=("parallel",), vmem_limit_bytes=VMEM_LIMIT),
        name="out_ffn",
    )(x2d, mix2d, w_out, g_ffn, w_fi, w_fo, g_final)


def _mixer_constants():
    L = CHUNK
    e3 = np.zeros((128, SSD_WIDTH), np.float32)
    for piece in range(3):
        for h in range(SSD_HEADS):
            e3[piece * SSD_HEADS + h, h * SSD_HEADDIM:(h + 1) * SSD_HEADDIM] = 1.0

    l = np.arange(L)[:, None]
    c = np.arange(SSD_WIDTH)[None, :]
    dsel = ((c % L) == l).astype(np.float32)
    s2 = np.arange(128)[None, :] % L
    causal2 = (s2 <= l).astype(np.float32)

    t = np.arange(L)
    blocks = []
    masks = []
    isq = np.zeros((L, 128), np.float32)
    for li, m in enumerate(HG_LEVELS):
        cm = np.zeros((L, L), np.float32)
        start = (t // (2 * m)) * (2 * m)
        mid = start + m - 1
        query = (t - start) >= m
        for s in range(L):
            if query[s]:
                cm[s, mid[s] + 1:s + 1] = 1.0
            else:
                cm[s, s + 1:mid[s] + 1] = 1.0
        blocks.append(cm)
        same = (start[:, None] == start[None, :])
        mk = same & query[:, None] & (~query[None, :])
        masks.append(np.concatenate([mk, mk], axis=1).astype(np.float32))
        isq[:, li] = query.astype(np.float32)
    tri = (t[None, :] <= t[:, None]).astype(np.float32)
    blocks.append(tri)
    blocks.append((t[None, :] > t[:, None]).astype(np.float32))
    ce = np.concatenate(blocks, axis=0)
    ce3 = np.concatenate([ce, ce, ce], axis=1)
    diag = (t[:, None] == t[None, :])
    masks.append(np.concatenate([diag, diag], axis=1).astype(np.float32))
    hmask = np.stack(masks, axis=0)
    return (jnp.asarray(e3, jnp.bfloat16), jnp.asarray(dsel), jnp.asarray(causal2),
            jnp.asarray(ce3, jnp.bfloat16), jnp.asarray(hmask), jnp.asarray(isq))


def _scan_time(x, op, identity, row):
    sh = 1
    while sh < x.shape[0]:
        r = pltpu.roll(x, sh, axis=0)
        x = op(x, jnp.where(row >= sh, r, identity))
        sh *= 2
    return x


def _blockdiag_rows(a, b):
    z = jnp.zeros_like(a)
    return jnp.concatenate([jnp.concatenate([a, z], axis=1), jnp.concatenate([z, b], axis=1)], axis=0)


def _mixer_chunk(col, put, sp_ref, cw_ref, cb_ref, dsk_ref, sg_ref, mlg_ref, hgg_ref,
                 e3_ref, dsel_ref, causal_ref, ce3_ref, hmask_ref, isq_ref,
                 mc_ref, mn_ref, mm_ref, ht_ref, hgt_ref, cbuf_ref, *, t_valid):
    L = CHUNK
    row = lax.broadcasted_iota(jnp.int32, (L, 128), 0)
    lane = lax.broadcasted_iota(jnp.int32, (L, 128), 1)
    lo_half = lane < 64
    causal2 = causal_ref[...] > 0.5

    nl = len(HG_LEVELS)
    ml_pairs = range(ML_HEADS // 2)
    hg_pairs = range(HG_HEADS // 2)

    sm = col(C_SM, 128)
    dt = _softplus(sm + sp_ref[0:1, :])
    a_neg2 = jnp.where(lane[0:1] < SSD_HEADS, -LOG2E * jnp.exp(sp_ref[1:2, :]), 0.0)
    ig = pltpu.roll(sm, LANE_MF - LANE_MI, axis=1) + sp_ref[2:3, :]
    lf = -_softplus(-(sm + sp_ref[3:4, :]))
    if t_valid < L:
        ok = row < t_valid
        dt = jnp.where(ok, dt, 0.0)
        ig = jnp.where(ok, ig, NEG_INF)
        lf = jnp.where(ok, lf, 0.0)
    ml_lane = (lane >= LANE_MF) & (lane < LANE_MF + ML_HEADS)
    cs = _scan_time(jnp.where(lane < SSD_HEADS, dt * a_neg2, jnp.where(ml_lane, lf, 0.0)), jnp.add, 0.0, row)
    acum2 = cs
    u = jnp.where(ml_lane, ig - cs, 0.0)
    m_prev = mm_ref[...]
    m_run = jnp.maximum(_scan_time(u, jnp.maximum, NEG_INF, row), m_prev)
    m_i = cs + m_run
    m_last = m_run[L - 1:L, :]
    w_inter = jnp.exp(m_prev - m_run)
    inv_floor = jnp.exp(-m_i)
    wk_s = jnp.exp(u - m_last)
    sc_s = jnp.exp(m_prev - m_last)

    hi, mid, lo = _split3(jnp.concatenate([dt, acum2], axis=0))
    lane2 = lax.broadcasted_iota(jnp.int32, (2 * L, 128), 1)
    packed = jnp.where(
        lane2 < SSD_HEADS, hi.astype(jnp.float32),
        jnp.where(lane2 < 2 * SSD_HEADS, pltpu.roll(mid.astype(jnp.float32), SSD_HEADS, axis=1),
                  jnp.where(lane2 < 3 * SSD_HEADS, pltpu.roll(lo.astype(jnp.float32), 2 * SSD_HEADS, axis=1), 0.0)))
    ex = _dot(_bf(packed), e3_ref[...])
    glog2 = col(C_HL, HG_WIDTH)
    if t_valid < L:
        glog2 = jnp.where(lax.broadcasted_iota(jnp.int32, (L, HG_WIDTH), 0) < t_valid, glog2, 0.0)
    ghi, gmid, glo = _split3(glog2)
    exps = _dot(ce3_ref[...], jnp.concatenate([ghi, gmid, glo], axis=0))
    yield

    xbc = col(C_XBC, CONV_CH)
    cbuf_ref[8:8 + L, :] = xbc
    xbc_a = _silu(_causal_conv(cbuf_ref, cw_ref, cb_ref, L))
    cbuf_ref[5:8, :] = xbc[t_valid - 3:t_valid, :]
    xs = xbc_a[:, 0:SSD_WIDTH]
    yield

    ml_q2 = [_bf(col(C_MQ + pp * 256, 256)) for pp in ml_pairs]
    ml_k = [col(C_MK + h * 128, 128) for h in range(ML_HEADS)]
    ml_v = [col(C_MV + h * 128, 128) for h in range(ML_HEADS)]
    ml_s2raw = [_dot_nt(ml_q2[pp], _bf(_blockdiag_rows(ml_k[2 * pp], ml_k[2 * pp + 1]))) for pp in ml_pairs]
    ml_qc = [_dot(ml_q2[h // 2][:, (h % 2) * 128:(h % 2 + 1) * 128], _bf(mc_ref[h])) for h in range(ML_HEADS)]

    kk = col(C_HK, HG_WIDTH)
    qq = col(C_HQ, HG_WIDTH)
    vv = col(C_HI, HG_WIDTH)
    qq_b, kk_b, vv_b = _bf(qq), _bf(kk), _bf(vv)
    hg_sl = [slice(h * 128, (h + 1) * 128) for h in range(HG_HEADS)]
    hg_adiag = [_dot_nt(qq_b[:, pp * 256:(pp + 1) * 256],
                        _blockdiag_rows(kk_b[:, hg_sl[2 * pp]], kk_b[:, hg_sl[2 * pp + 1]])) for pp in hg_pairs]

    ssd_bm, ssd_cb2, ssd_yi = [], [], []
    for g in range(SSD_GROUPS):
        bm = xbc_a[:, SSD_WIDTH + g * SSD_STATE:SSD_WIDTH + (g + 1) * SSD_STATE]
        cm = xbc_a[:, SSD_WIDTH + (SSD_GROUPS + g) * SSD_STATE:SSD_WIDTH + (SSD_GROUPS + g + 1) * SSD_STATE]
        bm_b, cm_b = _bf(bm), _bf(cm)
        ssd_bm.append(bm)
        ssd_cb2.append(_dot_nt(cm_b, jnp.concatenate([bm_b, bm_b], axis=0)))
        ssd_yi.append(_dot(cm_b, _bf(ht_ref[:, g * 512:(g + 1) * 512])))
    yield

    dt_f, ac_f = ex[0:L], ex[L:2 * L]
    eac_f = jnp.exp2(ac_f)
    dend_f = jnp.exp2(ac_f[L - 1:L, :] - ac_f)
    xt = xs * dt_f
    ctr = jnp.sum(jnp.where(dsel_ref[...] > 0.5, ac_f, 0.0), axis=0, keepdims=True)
    xdec = _bf(xt * dend_f)
    eac_end = eac_f[L - 1:L, :]
    yield

    qk_lv = []
    for li in range(nl):
        isq = isq_ref[:, li:li + 1] > 0.5
        qk_lv.append(_bf(jnp.where(isq, qq, kk) * jnp.exp2(exps[li * L:(li + 1) * L])))
    gc = exps[nl * L:(nl + 1) * L]
    gend = exps[(nl + 1) * L:(nl + 2) * L]
    q_dec = _bf(qq * jnp.exp2(gc))
    k_dec = _bf(kk * jnp.exp2(gend))
    s_dec = jnp.exp2(gc[L - 1:L, :])
    yield

    hg_masks = [hmask_ref[li] > 0.5 for li in range(nl + 1)]
    hg_a2 = []
    for pp in hg_pairs:
        a2 = jnp.where(hg_masks[nl], hg_adiag[pp], 0.0)
        for li in range(nl):
            x = qk_lv[li]
            a_l = _dot_nt(x[:, pp * 256:(pp + 1) * 256], _blockdiag_rows(x[:, hg_sl[2 * pp]], x[:, hg_sl[2 * pp + 1]]))
            a2 = jnp.where(hg_masks[li], a_l, a2)
        hg_a2.append(_bf(a2))
    hg_oi = [_dot_nt(q_dec[:, hg_sl[h]], _bf(hgt_ref[h])) for h in range(HG_HEADS)]
    yield

    dsel128 = dsel_ref[:, 0:128] > 0.5
    ml_s2, ml_dens = [], []
    for pp in ml_pairs:
        l0, l1 = LANE_MF + 2 * pp, LANE_MF + 2 * pp + 1
        m_row = jnp.where(lo_half, m_run[:, l0:l0 + 1], m_run[:, l1:l1 + 1])
        u_row = jnp.where(lo_half, u[:, l0:l0 + 1], u[:, l1:l1 + 1])
        u_col = jnp.sum(jnp.where(dsel128, u_row, 0.0), axis=0, keepdims=True)
        s2 = ml_s2raw[pp] * jnp.exp(jnp.where(causal2, u_col - m_row, NEG_INF))
        ml_s2.append(s2)
        ml_dens.append((jnp.sum(jnp.where(lo_half, s2, 0.0), axis=-1, keepdims=True),
                        jnp.sum(jnp.where(lo_half, 0.0, s2), axis=-1, keepdims=True)))
    ml_num2 = [_dot(_bf(ml_s2[pp]), _bf(_blockdiag_rows(ml_v[2 * pp], ml_v[2 * pp + 1]))) for pp in ml_pairs]
    yield

    y_parts = []
    for g in range(SSD_GROUPS):
        gs = slice(g * 512, (g + 1) * 512)
        for pp in range(4):
            ps = slice((g * 4 + pp) * 128, (g * 4 + pp + 1) * 128)
            seg = ac_f[:, ps] - ctr[:, ps]
            m2 = _bf(ssd_cb2[g] * jnp.exp2(jnp.where(causal2, seg, NEG_INF)))
            xp = xt[:, ps]
            xb = _bf(jnp.concatenate([jnp.where(lo_half, xp, 0.0), jnp.where(lo_half, 0.0, xp)], axis=0))
            y_parts.append(_dot(m2, xb) + eac_f[:, ps] * ssd_yi[g][:, pp * 128:(pp + 1) * 128])
        ht_ref[:, gs] = ht_ref[:, gs] * eac_end[:, gs] + _dot(_bf(ssd_bm[g].T), xdec[:, gs])
        yield

    hg_o2 = [_dot(hg_a2[pp], _blockdiag_rows(vv_b[:, hg_sl[2 * pp]], vv_b[:, hg_sl[2 * pp + 1]]))
             for pp in hg_pairs]

    for h in range(ML_HEADS):
        pp, j, ln = h // 2, h % 2, LANE_MF + h
        qh = col(C_MQ + h * 128, 128)
        wi = w_inter[:, ln:ln + 1]
        num = ml_num2[pp][:, j * 128:(j + 1) * 128] + wi * ml_qc[h]
        qn = jnp.sum(qh * mn_ref[h:h + 1, :], axis=-1, keepdims=True)
        den = ml_dens[pp][j] + wi * qn
        hval = num * (1.0 / jnp.maximum(jnp.abs(den), inv_floor[:, ln:ln + 1]))
        hn = _rms(hval, mlg_ref[:, h * 128:(h + 1) * 128])
        put(SSD_WIDTH + h * 128, _bf(col(C_MO + h * 128, 128) * hn))
        kw = ml_k[h] * wk_s[:, ln:ln + 1]
        sc = sc_s[:, ln:ln + 1]
        mc_ref[h] = sc * mc_ref[h] + _dot(_bf(kw.T), _bf(ml_v[h]))
        mn_ref[h:h + 1, :] = sc * mn_ref[h:h + 1, :] + jnp.sum(kw, axis=0, keepdims=True)
        if h % 2 == 1:
            yield
    mm_ref[...] = jnp.where(ml_lane[0:1], m_i[L - 1:L, :], 0.0)

    y = jnp.concatenate(y_parts, axis=1) + dsk_ref[...] * xs
    y_ssd = _rms(y * col(C_Z, SSD_WIDTH), sg_ref[...])
    put(0, _bf(y_ssd))
    yield

    for h in range(HG_HEADS):
        hs = hg_sl[h]
        o = hg_o2[h // 2][:, (h % 2) * 128:(h % 2 + 1) * 128] + hg_oi[h]
        on = _rms(o, hgg_ref[:, hs])
        put(SSD_WIDTH + ML_WIDTH + h * 128, _bf(on * col(C_HG + h * 128, 128)))
        hgt_ref[h] = hgt_ref[h] * s_dec[:, hs] + _dot(_bf(vv[:, hs].T), k_dec[:, hs])


def _lockstep(gens):
    gens = list(gens)
    while gens:
        alive = []
        for g in gens:
            try:
                next(g)
                alive.append(g)
            except StopIteration:
                pass
        gens = alive


def _mixer_kernel(*refs, t_rows, n_chunks, n_seq, layer, depth, has_state, has_prev):
    it = iter(refs)
    proj_ref = next(it)
    state0 = [next(it) for _ in range(6)] if has_state else None
    sp_ref, cw_ref, cb_ref, dsk_ref, sg_ref, mlg_ref, hgg_ref = (next(it) for _ in range(7))
    e3_ref, dsel_ref, causal_ref, ce3_ref, hmask_ref, isq_ref = (next(it) for _ in range(6))
    if has_prev:
        for _ in range(6):
            next(it)
    mix_ref, conv_ref, ssd_ref, mc_ref, mn_ref, mm_ref, hg_ref = (next(it) for _ in range(7))
    cbuf_ref, ht_ref, hgt_ref = (next(it) for _ in range(3))
    padded = t_rows < CHUNK
    pbuf_ref = next(it) if padded else None
    tb = pl.program_id(1)
    li = 0 if has_prev else layer
    out_refs = (conv_ref, ssd_ref, mc_ref, mn_ref, mm_ref, hg_ref)

    @pl.when(tb == 0)
    def _init():
        if not has_prev:
            for other in range(depth):
                if other != layer:
                    for r in out_refs:
                        r[other] = jnp.zeros(r.shape[1:], jnp.float32)
        for s in range(n_seq):
            if has_state:
                conv0_ref, ssd0_ref, mc0_ref, mn0_ref, mm0_ref, hg0_ref = state0
                cbuf_ref[s, 5:8, :] = conv0_ref[0, s]
                ht_ref[s] = ssd0_ref[0, s].reshape(SSD_WIDTH, SSD_STATE).T
                mc_ref[li, s] = mc0_ref[0, s]
                mn_ref[li, s] = mn0_ref[0, s]
                mm_ref[li, s] = mm0_ref[0, s]
                for h in range(HG_HEADS):
                    hgt_ref[s, h] = hg0_ref[0, s, h].T
            else:
                cbuf_ref[s, 5:8, :] = jnp.zeros((CONV_W - 1, CONV_CH), jnp.float32)
                ht_ref[s] = jnp.zeros(ht_ref.shape[1:], jnp.float32)
                mc_ref[li, s] = jnp.zeros(mc_ref.shape[2:], jnp.float32)
                mn_ref[li, s] = jnp.zeros(mn_ref.shape[2:], jnp.float32)
                mm_ref[li, s] = jnp.zeros(mm_ref.shape[2:], jnp.float32)
                hgt_ref[s] = jnp.zeros(hgt_ref.shape[1:], jnp.float32)

    def chunk(s, col, put):
        return _mixer_chunk(
            col, put, sp_ref=sp_ref, cw_ref=cw_ref, cb_ref=cb_ref, dsk_ref=dsk_ref, sg_ref=sg_ref, mlg_ref=mlg_ref,
            hgg_ref=hgg_ref, e3_ref=e3_ref, dsel_ref=dsel_ref, causal_ref=causal_ref, ce3_ref=ce3_ref,
            hmask_ref=hmask_ref, isq_ref=isq_ref, mc_ref=mc_ref.at[li, s], mn_ref=mn_ref.at[li, s],
            mm_ref=mm_ref.at[li, s], ht_ref=ht_ref.at[s], hgt_ref=hgt_ref.at[s], cbuf_ref=cbuf_ref.at[s],
            t_valid=t_rows if padded else CHUNK)

    if padded:
        gens = []
        for s in range(n_seq):
            pbuf_ref[s, 0:t_rows, :] = proj_ref[s]
            pbuf_ref[s, t_rows:CHUNK, :] = jnp.zeros((CHUNK - t_rows, PROJ_COLS), jnp.float32)

            def col(c0, width, s=s):
                return pbuf_ref[s, :, c0:c0 + width]

            def put(c0, val, s=s):
                mix_ref[s, :, c0:c0 + val.shape[1]] = val[0:t_rows]

            gens.append(chunk(s, col, put))
        _lockstep(gens)
    else:
        def body(ci, carry):
            rows = pl.ds(pl.multiple_of(ci * CHUNK, CHUNK), CHUNK)
            gens = []
            for s in range(n_seq):
                def col(c0, width, s=s):
                    return proj_ref[s, rows, c0:c0 + width]

                def put(c0, val, s=s):
                    mix_ref[s, rows, c0:c0 + val.shape[1]] = val

                gens.append(chunk(s, col, put))
            _lockstep(gens)
            return carry
        lax.fori_loop(0, n_chunks, body, 0, unroll=2 if n_chunks % 2 == 0 else 1)

    @pl.when(tb == pl.num_programs(1) - 1)
    def _fin():
        for s in range(n_seq):
            conv_ref[li, s] = cbuf_ref[s, 5:8, :]
            ssd_ref[li, s] = ht_ref[s].T.reshape(SSD_HEADS, SSD_HEADDIM, SSD_STATE)
            for h in range(HG_HEADS):
                hg_ref[li, s, h] = hgt_ref[s, h].T


def _mixer(proj, state_in, prev, params, consts, layer, depth):
    b, t, _ = proj.shape
    if t < CHUNK:
        tblk = t
    else:
        tblk = min(TIME_BLOCK, t)
        assert t % tblk == 0 and tblk % CHUNK == 0
    ns = SEQ_PER_STEP
    assert b % ns == 0
    c2 = lambda i, j: (0, 0)
    c3 = lambda i, j: (0, 0, 0)
    state_dims = [(CONV_W - 1, CONV_CH), (SSD_HEADS, SSD_HEADDIM, SSD_STATE), (ML_HEADS, ML_HD, ML_HD),
                  (ML_HEADS, ML_HD), (1, 128), (HG_HEADS, HG_HD, HG_HD)]

    def state_spec(dims, layers, first):
        return pl.BlockSpec((layers, ns) + dims, lambda i, j: (first, i) + (0,) * len(dims))

    in_state_specs = [state_spec(d, 1, layer) for d in state_dims]
    out_state_specs = [state_spec(d, 1, layer) if prev is not None else state_spec(d, depth, 0) for d in state_dims]
    state_shapes = [
        jax.ShapeDtypeStruct((depth, b, CONV_W - 1, CONV_CH), jnp.float32),
        jax.ShapeDtypeStruct((depth, b, SSD_HEADS, SSD_HEADDIM, SSD_STATE), jnp.float32),
        jax.ShapeDtypeStruct((depth, b, ML_HEADS, ML_HD, ML_HD), jnp.float32),
        jax.ShapeDtypeStruct((depth, b, ML_HEADS, ML_HD), jnp.float32),
        jax.ShapeDtypeStruct((depth, b, 1, 128), jnp.float32),
        jax.ShapeDtypeStruct((depth, b, HG_HEADS, HG_HD, HG_HD), jnp.float32),
    ]
    args = [proj]
    in_specs = [pl.BlockSpec((ns, tblk, PROJ_COLS), lambda i, j: (i, j, 0))]
    if state_in is not None:
        args += list(state_in)
        in_specs += in_state_specs
    args += list(params) + list(consts)
    in_specs += [pl.BlockSpec(a.shape, c3 if a.ndim == 3 else c2) for a in list(params) + list(consts)]
    aliases = {}
    if prev is not None:
        for k, a in enumerate(prev):
            aliases[len(args)] = 1 + k
            args.append(a)
            in_specs.append(pl.BlockSpec(memory_space=pl.ANY))
    scratch = [pltpu.VMEM((ns, 8 + CHUNK, CONV_CH), jnp.float32),
               pltpu.VMEM((ns, SSD_STATE, SSD_WIDTH), jnp.float32),
               pltpu.VMEM((ns, HG_HEADS, HG_HD, HG_HD), jnp.float32)]
    if t < CHUNK:
        scratch.append(pltpu.VMEM((ns, CHUNK, PROJ_COLS), jnp.float32))
    return pl.pallas_call(
        functools.partial(_mixer_kernel, t_rows=tblk, n_chunks=max(tblk // CHUNK, 1), n_seq=ns, layer=layer,
                          depth=depth, has_state=state_in is not None, has_prev=prev is not None),
        grid=(b // ns, t // tblk),
        in_specs=in_specs,
        out_specs=[pl.BlockSpec((ns, tblk, D_MIX), lambda i, j: (i, j, 0))] + out_state_specs,
        out_shape=[jax.ShapeDtypeStruct((b, t, D_MIX), jnp.bfloat16)] + state_shapes,
        scratch_shapes=scratch,
        input_output_aliases=aliases,
        compiler_params=pltpu.CompilerParams(
            dimension_semantics=("parallel", "arbitrary"), vmem_limit_bytes=VMEM_LIMIT),
        name="mixer",
    )(*args)


def _regroup_w_in(w):
    o = {}
    acc = 0
    for name, size in (("z", 1024), ("xbc", 1536), ("dt", 16), ("mq", 512), ("mk", 512), ("mv", 512),
                       ("mi", 4), ("mf", 4), ("mo", 512), ("hq", 512), ("hf", 512), ("hi", 512), ("hg", 512)):
        o[name] = w[..., acc:acc + size]
        acc += size
    pad = jnp.zeros(w.shape[:-1] + (W_COLS - C_SM - 24,), w.dtype)
    cols = [o["z"], o["xbc"], o["mq"], o["mk"], o["mv"], o["mo"], o["hq"], o["hf"], o["hi"], o["hg"],
            o["dt"], o["mi"], o["mf"], pad]
    return jnp.concatenate(cols, axis=-1).astype(jnp.bfloat16)


def _small_params(dt_bias, a_log, ml_bi, ml_bf):
    sp = jnp.zeros((8, 128), jnp.float32)
    sp = sp.at[0, LANE_DT:LANE_DT + SSD_HEADS].set(dt_bias)
    sp = sp.at[1, LANE_DT:LANE_DT + SSD_HEADS].set(a_log)
    sp = sp.at[2, LANE_MF:LANE_MF + ML_HEADS].set(ml_bi)
    sp = sp.at[3, LANE_MF:LANE_MF + ML_HEADS].set(ml_bf)
    return sp


def _run_trunk(x, state_in, lp, norm_final, consts, tm_proj, tm_ffn):
    b, t, _ = x.shape
    assert t % CHUNK == 0 or t < CHUNK, "a partial chunk is only supported for single-chunk sequences"
    depth = len(lp)
    x2d = x.reshape(b * t, D_MODEL)
    states = None
    for l in range(depth):
        p = lp[l]
        params = (p["sp"], p["conv_w"], p["conv_b"], p["d_skip"], p["ssd_gain"], p["ml_gain"], p["hg_gain"])
        proj = _in_proj(x2d, p["norm_mix"], p["w_in"], p["hg_lb"], tm_proj, l)
        res = _mixer(proj.reshape(b, t, PROJ_COLS), state_in, states, params, consts, l, depth)
        states = res[1:]
        x2d = _out_ffn(x2d, res[0].reshape(b * t, D_MIX), p["w_out"], p["norm_ffn"], p["w_ffn_in"],
                       p["w_ffn_out"], norm_final, tm_ffn, l, l == depth - 1)
    conv, ssd, mc, mn, mm_p, hg = states
    return x2d.reshape(b, t, D_MODEL), (conv, ssd, mc, mn, mm_p[:, :, 0, LANE_MF:LANE_MF + ML_HEADS], hg)


def kernel(x_prompt, x_sample, state_conv, state_ssd, state_mlstm_c, state_mlstm_n, state_mlstm_m, state_hgrn,
           norm_mix, w_in, conv_w, conv_b, dt_bias, a_log, d_skip, ssd_gain, ml_bi, ml_bf, ml_gain,
           hg_lb, hg_gain, w_out, norm_ffn, w_ffn_in, w_ffn_out, norm_final):
    depth = w_in.shape[0]
    f32 = jnp.float32
    consts = _mixer_constants()
    bf16 = jnp.bfloat16
    w_in_b, w_out_b, w_fi_b, w_fo_b = _regroup_w_in(w_in), w_out.astype(bf16), w_ffn_in.astype(bf16), w_ffn_out.astype(bf16)
    lp = []
    for l in range(depth):
        lp.append({
            "norm_mix": norm_mix[l].reshape(1, D_MODEL),
            "w_in": w_in_b,
            "sp": _small_params(dt_bias[l], a_log[l], ml_bi[l], ml_bf[l]),
            "conv_w": conv_w[l],
            "conv_b": conv_b[l].reshape(1, CONV_CH),
            "d_skip": jnp.repeat(d_skip[l], SSD_HEADDIM).reshape(1, SSD_WIDTH),
            "ssd_gain": ssd_gain[l].reshape(1, SSD_WIDTH),
            "ml_gain": ml_gain[l].reshape(1, ML_WIDTH),
            "hg_lb": hg_lb.astype(f32),
            "hg_gain": hg_gain[l].reshape(1, HG_WIDTH),
            "w_out": w_out_b,
            "norm_ffn": norm_ffn[l].reshape(1, D_MODEL),
            "w_ffn_in": w_fi_b,
            "w_ffn_out": w_fo_b,
        })
    nf = norm_final.reshape(1, D_MODEL)

    y_prompt, p_outs = _run_trunk(x_prompt, None, lp, nf, consts, 256, 512)
    bs = x_sample.shape[0]
    mm_p = jnp.zeros((depth, bs, 1, 128), f32).at[:, :, 0, LANE_MF:LANE_MF + ML_HEADS].set(state_mlstm_m)
    s_in = (state_conv, state_ssd, state_mlstm_c, state_mlstm_n, mm_p, state_hgrn)
    y_sample, s_outs = _run_trunk(x_sample, s_in, lp, nf, consts, 256, 512)
    return (y_prompt, y_sample) + p_outs + s_outs
```

```python
import functools

import numpy as np
import jax
import jax.numpy as jnp
from jax import lax
from jax.experimental import pallas as pl
from jax.experimental.pallas import tpu as pltpu

D_MODEL = 1024
CHUNK = 64
SSD_WIDTH = 1024
SSD_HEADDIM = 64
SSD_HEADS = 16
SSD_GROUPS = 2
SSD_STATE = 128
CONV_W = 4
CONV_CH = SSD_WIDTH + 2 * SSD_GROUPS * SSD_STATE
ML_WIDTH = 512
ML_HEADS = 4
ML_HD = 128
HG_WIDTH = 512
HG_HEADS = 4
HG_HD = 128
D_MIX = 2048
D_FF = 2816
EPS = 1e-6

C_Z = 0
C_XBC = 1024
C_MQ, C_MK, C_MV, C_MO = 2560, 3072, 3584, 4096
C_HQ, C_HK, C_HI, C_HG = 4608, 5120, 5632, 6144
C_SM = 6656
W_COLS = 6784
C_HL = 6784
PROJ_COLS = 7296
TIME_BLOCK = 256
SEQ_PER_STEP = 1
LOG2E = 1.4426950408889634
LANE_DT = 0
LANE_MI = 16
LANE_MF = 20

VMEM_LIMIT = 52 * 1024 * 1024
NEG_INF = float("-inf")

HG_LEVELS = (32, 16, 8, 4, 2, 1)

_NT = (((1,), (1,)), ((), ()))


def _bf(x):
    return x.astype(jnp.bfloat16)


def _dot(a, b):
    return jnp.dot(a, b, preferred_element_type=jnp.float32)


def _dot_nt(a, b):
    return lax.dot_general(a, b, _NT, preferred_element_type=jnp.float32)


def _sigmoid(x):
    return 1.0 / (1.0 + jnp.exp(-x))


def _silu(x):
    return x * _sigmoid(x)


def _softplus(x):
    return jnp.maximum(x, 0.0) + jnp.log(1.0 + jnp.exp(-jnp.abs(x)))


def _split3(x):
    hi = _bf(x)
    r1 = x - hi.astype(jnp.float32)
    mid = _bf(r1)
    r2 = r1 - mid.astype(jnp.float32)
    return hi, mid, _bf(r2)


def _rms(x, gain):
    return x * lax.rsqrt(jnp.mean(x * x, axis=-1, keepdims=True) + EPS) * gain


def _hg_lower_bound(lb_all, layer):
    lb_e = jnp.exp(lb_all - jnp.max(lb_all, axis=0, keepdims=True))
    lb_soft = lb_e * (1.0 / jnp.sum(lb_e, axis=0, keepdims=True))
    return jnp.sum(lb_soft[0:layer + 1], axis=0, keepdims=True) - lb_soft[0:1]


def _causal_conv(cbuf_ref, cw_ref, cb_ref, rows):
    acc = cb_ref[...] + cbuf_ref[8:8 + rows, :] * cw_ref[CONV_W - 1:CONV_W, :]
    for w in range(CONV_W - 1):
        acc = acc + cbuf_ref[5 + w:5 + w + rows, :] * cw_ref[w:w + 1, :]
    return acc


_PROJ_GROUPS = (
    (C_Z, 512, "silu"), (C_Z + 512, 512, "silu"),
    (C_XBC, 512, None), (C_XBC + 512, 512, None), (C_XBC + 1024, 512, None),
    (C_MQ, 512, None), (C_MK, 512, "kscale"), (C_MV, 512, None), (C_MO, 512, "sigmoid"),
    (C_HQ, 512, "silu"), (C_HK, 512, "fgate"), (C_HI, 512, None), (C_HG, 512, "silu"),
    (C_SM, 128, None),
)


def _proj_group(h, w_ref, lb_ref, layer, group, store):
    c0, width, act = group
    acc = _dot(h, w_ref[:, c0:c0 + width])
    if act == "silu":
        acc = _silu(acc)
    elif act == "sigmoid":
        acc = _sigmoid(acc)
    elif act == "kscale":
        acc = acc * (ML_HD ** -0.5)
    elif act == "fgate":
        lb = _hg_lower_bound(lb_ref[...], layer)
        fg = lb + (1.0 - lb) * _sigmoid(acc)
        store(C_HL, jnp.log2(fg))
        acc = 1.0 - fg
    store(c0, acc)


def _in_proj_kernel(x_ref, g_ref, w_ref, lb_ref, o_ref, *, layer):
    h = _bf(_rms(x_ref[...], g_ref[...]))

    def store(c0, val):
        o_ref[:, c0:c0 + val.shape[1]] = val

    for group in _PROJ_GROUPS:
        _proj_group(h, w_ref, lb_ref, layer, group, store)


def _in_proj(x2d, gain, w, hg_lb, tm, layer):
    n = x2d.shape[0]
    c2 = lambda i: (0, 0)
    return pl.pallas_call(
        functools.partial(_in_proj_kernel, layer=layer),
        grid=(n // tm,),
        in_specs=[
            pl.BlockSpec((tm, D_MODEL), lambda i: (i, 0)),
            pl.BlockSpec((1, D_MODEL), c2),
            pl.BlockSpec((None, D_MODEL, W_COLS), lambda i: (layer, 0, 0), pipeline_mode=pl.Buffered(1)),
            pl.BlockSpec(hg_lb.shape, c2),
        ],
        out_specs=pl.BlockSpec((tm, PROJ_COLS), lambda i: (i, 0)),
        out_shape=jax.ShapeDtypeStruct((n, PROJ_COLS), jnp.float32),
        compiler_params=pltpu.CompilerParams(
            dimension_semantics=("parallel",), vmem_limit_bytes=VMEM_LIMIT),
        name="in_proj",
    )(x2d, gain, w, hg_lb)


FF_TILE = 256


def _out_ffn_kernel(x_ref, mix_ref, wo_ref, gn_ref, wi_ref, wf_ref, gf_ref, o_ref, *, final_norm):
    x1 = x_ref[...] + _dot(mix_ref[...], wo_ref[...])
    h2 = _bf(_rms(x1, gn_ref[...]))
    acc = x1
    for j in range(D_FF // FF_TILE):
        g = _dot(h2, wi_ref[:, j * FF_TILE:(j + 1) * FF_TILE])
        u = _dot(h2, wi_ref[:, D_FF + j * FF_TILE:D_FF + (j + 1) * FF_TILE])
        acc = acc + _dot(_bf(_silu(g) * u), wf_ref[j * FF_TILE:(j + 1) * FF_TILE, :])
    if final_norm:
        acc = _rms(acc, gf_ref[...])
    o_ref[...] = acc


def _out_ffn(x2d, mix2d, w_out, g_ffn, w_fi, w_fo, g_final, tm, layer, final_norm):
    n = x2d.shape[0]
    const = lambda i: (0, 0)
    lyr = lambda i: (layer, 0, 0)
    one = pl.Buffered(1)
    return pl.pallas_call(
        functools.partial(_out_ffn_kernel, final_norm=final_norm),
        grid=(n // tm,),
        in_specs=[
            pl.BlockSpec((tm, D_MODEL), lambda i: (i, 0)),
            pl.BlockSpec((tm, D_MIX), lambda i: (i, 0)),
            pl.BlockSpec((None, D_MIX, D_MODEL), lyr, pipeline_mode=one),
            pl.BlockSpec((1, D_MODEL), const),
            pl.BlockSpec((None, D_MODEL, 2 * D_FF), lyr, pipeline_mode=one),
            pl.BlockSpec((None, D_FF, D_MODEL), lyr, pipeline_mode=one),
            pl.BlockSpec((1, D_MODEL), const),
        ],
        out_specs=pl.BlockSpec((tm, D_MODEL), lambda i: (i, 0)),
        out_shape=jax.ShapeDtypeStruct((n, D_MODEL), jnp.float32),
        compiler_params=pltpu.CompilerParams(
            dimension_semantics=("parallel",), vmem_limit_bytes=VMEM_LIMIT),
        name="out_ffn",
    )(x2d, mix2d, w_out, g_ffn, w_fi, w_fo, g_final)


def _mixer_constants():
    L = CHUNK
    e3 = np.zeros((128, SSD_WIDTH), np.float32)
    for piece in range(3):
        for h in range(SSD_HEADS):
            e3[piece * SSD_HEADS + h, h * SSD_HEADDIM:(h + 1) * SSD_HEADDIM] = 1.0

    l = np.arange(L)[:, None]
    c = np.arange(SSD_WIDTH)[None, :]
    dsel = ((c % L) == l).astype(np.float32)
    s2 = np.arange(128)[None, :] % L
    causal2 = (s2 <= l).astype(np.float32)

    t = np.arange(L)
    blocks = []
    masks = []
    isq = np.zeros((L, 128), np.float32)
    for li, m in enumerate(HG_LEVELS):
        cm = np.zeros((L, L), np.float32)
        start = (t // (2 * m)) * (2 * m)
        mid = start + m - 1
        query = (t - start) >= m
        for s in range(L):
            if query[s]:
                cm[s, mid[s] + 1:s + 1] = 1.0
            else:
                cm[s, s + 1:mid[s] + 1] = 1.0
        blocks.append(cm)
        same = (start[:, None] == start[None, :])
        mk = same & query[:, None] & (~query[None, :])
        masks.append(np.concatenate([mk, mk], axis=1).astype(np.float32))
        isq[:, li] = query.astype(np.float32)
    tri = (t[None, :] <= t[:, None]).astype(np.float32)
    blocks.append(tri)
    blocks.append((t[None, :] > t[:, None]).astype(np.float32))
    ce = np.concatenate(blocks, axis=0)
    ce3 = np.concatenate([ce, ce, ce], axis=1)
    diag = (t[:, None] == t[None, :])
    masks.append(np.concatenate([diag, diag], axis=1).astype(np.float32))
    hmask = np.stack(masks, axis=0)
    return (jnp.asarray(e3, jnp.bfloat16), jnp.asarray(dsel), jnp.asarray(causal2),
            jnp.asarray(ce3, jnp.bfloat16), jnp.asarray(hmask), jnp.asarray(isq))


def _scan_time(x, op, identity, row):
    sh = 1
    while sh < x.shape[0]:
        r = pltpu.roll(x, sh, axis=0)
        x = op(x, jnp.where(row >= sh, r, identity))
        sh *= 2
    return x


def _blockdiag_rows(a, b):
    z = jnp.zeros_like(a)
    return jnp.concatenate([jnp.concatenate([a, z], axis=1), jnp.concatenate([z, b], axis=1)], axis=0)


def _mixer_chunk(col, put, sp_ref, cw_ref, cb_ref, dsk_ref, sg_ref, mlg_ref, hgg_ref,
                 e3_ref, dsel_ref, causal_ref, ce3_ref, hmask_ref, isq_ref,
                 mc_ref, mn_ref, mm_ref, ht_ref, hgt_ref, cbuf_ref, *, t_valid):
    L = CHUNK
    row = lax.broadcasted_iota(jnp.int32, (L, 128), 0)
    lane = lax.broadcasted_iota(jnp.int32, (L, 128), 1)
    lo_half = lane < 64
    causal2 = causal_ref[...] > 0.5

    nl = len(HG_LEVELS)
    ml_pairs = range(ML_HEADS // 2)
    hg_pairs = range(HG_HEADS // 2)

    sm = col(C_SM, 128)
    dt = _softplus(sm + sp_ref[0:1, :])
    a_neg2 = jnp.where(lane[0:1] < SSD_HEADS, -LOG2E * jnp.exp(sp_ref[1:2, :]), 0.0)
    ig = pltpu.roll(sm, LANE_MF - LANE_MI, axis=1) + sp_ref[2:3, :]
    lf = -_softplus(-(sm + sp_ref[3:4, :]))
    if t_valid < L:
        ok = row < t_valid
        dt = jnp.where(ok, dt, 0.0)
        ig = jnp.where(ok, ig, NEG_INF)
        lf = jnp.where(ok, lf, 0.0)
    ml_lane = (lane >= LANE_MF) & (lane < LANE_MF + ML_HEADS)
    cs = _scan_time(jnp.where(lane < SSD_HEADS, dt * a_neg2, jnp.where(ml_lane, lf, 0.0)), jnp.add, 0.0, row)
    acum2 = cs
    u = jnp.where(ml_lane, ig - cs, 0.0)
    m_prev = mm_ref[...]
    m_run = jnp.maximum(_scan_time(u, jnp.maximum, NEG_INF, row), m_prev)
    m_i = cs + m_run
    m_last = m_run[L - 1:L, :]
    w_inter = jnp.exp(m_prev - m_run)
    inv_floor = jnp.exp(-m_i)
    wk_s = jnp.exp(u - m_last)
    sc_s = jnp.exp(m_prev - m_last)

    hi, mid, lo = _split3(jnp.concatenate([dt, acum2], axis=0))
    lane2 = lax.broadcasted_iota(jnp.int32, (2 * L, 128), 1)
    packed = jnp.where(
        lane2 < SSD_HEADS, hi.astype(jnp.float32),
        jnp.where(lane2 < 2 * SSD_HEADS, pltpu.roll(mid.astype(jnp.float32), SSD_HEADS, axis=1),
                  jnp.where(lane2 < 3 * SSD_HEADS, pltpu.roll(lo.astype(jnp.float32), 2 * SSD_HEADS, axis=1), 0.0)))
    ex = _dot(_bf(packed), e3_ref[...])
    glog2 = col(C_HL, HG_WIDTH)
    if t_valid < L:
        glog2 = jnp.where(lax.broadcasted_iota(jnp.int32, (L, HG_WIDTH), 0) < t_valid, glog2, 0.0)
    ghi, gmid, glo = _split3(glog2)
    exps = _dot(ce3_ref[...], jnp.concatenate([ghi, gmid, glo], axis=0))
    yield

    xbc = col(C_XBC, CONV_CH)
    cbuf_ref[8:8 + L, :] = xbc
    xbc_a = _silu(_causal_conv(cbuf_ref, cw_ref, cb_ref, L))
    cbuf_ref[5:8, :] = xbc[t_valid - 3:t_valid, :]
    xs = xbc_a[:, 0:SSD_WIDTH]
    yield

    ml_q2 = [_bf(col(C_MQ + pp * 256, 256)) for pp in ml_pairs]
    ml_k = [col(C_MK + h * 128, 128) for h in range(ML_HEADS)]
    ml_v = [col(C_MV + h * 128, 128) for h in range(ML_HEADS)]
    ml_s2raw = [_dot_nt(ml_q2[pp], _bf(_blockdiag_rows(ml_k[2 * pp], ml_k[2 * pp + 1]))) for pp in ml_pairs]
    ml_qc = [_dot(ml_q2[h // 2][:, (h % 2) * 128:(h % 2 + 1) * 128], _bf(mc_ref[h])) for h in range(ML_HEADS)]

    kk = col(C_HK, HG_WIDTH)
    qq = col(C_HQ, HG_WIDTH)
    vv = col(C_HI, HG_WIDTH)
    qq_b, kk_b, vv_b = _bf(qq), _bf(kk), _bf(vv)
    hg_sl = [slice(h * 128, (h + 1) * 128) for h in range(HG_HEADS)]
    hg_adiag = [_dot_nt(qq_b[:, pp * 256:(pp + 1) * 256],
                        _blockdiag_rows(kk_b[:, hg_sl[2 * pp]], kk_b[:, hg_sl[2 * pp + 1]])) for pp in hg_pairs]

    ssd_bm, ssd_cb2, ssd_yi = [], [], []
    for g in range(SSD_GROUPS):
        bm = xbc_a[:, SSD_WIDTH + g * SSD_STATE:SSD_WIDTH + (g + 1) * SSD_STATE]
        cm = xbc_a[:, SSD_WIDTH + (SSD_GROUPS + g) * SSD_STATE:SSD_WIDTH + (SSD_GROUPS + g + 1) * SSD_STATE]
        bm_b, cm_b = _bf(bm), _bf(cm)
        ssd_bm.append(bm)
        ssd_cb2.append(_dot_nt(cm_b, jnp.concatenate([bm_b, bm_b], axis=0)))
        ssd_yi.append(_dot(cm_b, _bf(ht_ref[:, g * 512:(g + 1) * 512])))
    yield

    dt_f, ac_f = ex[0:L], ex[L:2 * L]
    eac_f = jnp.exp2(ac_f)
    dend_f = jnp.exp2(ac_f[L - 1:L, :] - ac_f)
    xt = xs * dt_f
    ctr = jnp.sum(jnp.where(dsel_ref[...] > 0.5, ac_f, 0.0), axis=0, keepdims=True)
    xdec = _bf(xt * dend_f)
    eac_end = eac_f[L - 1:L, :]
    yield

    qk_lv = []
    for li in range(nl):
        isq = isq_ref[:, li:li + 1] > 0.5
        qk_lv.append(_bf(jnp.where(isq, qq, kk) * jnp.exp2(exps[li * L:(li + 1) * L])))
    gc = exps[nl * L:(nl + 1) * L]
    gend = exps[(nl + 1) * L:(nl + 2) * L]
    q_dec = _bf(qq * jnp.exp2(gc))
    k_dec = _bf(kk * jnp.exp2(gend))
    s_dec = jnp.exp2(gc[L - 1:L, :])
    yield

    hg_masks = [hmask_ref[li] > 0.5 for li in range(nl + 1)]
    hg_a2 = []
    for pp in hg_pairs:
        a2 = jnp.where(hg_masks[nl], hg_adiag[pp], 0.0)
        for li in range(nl):
            x = qk_lv[li]
            a_l = _dot_nt(x[:, pp * 256:(pp + 1) * 256], _blockdiag_rows(x[:, hg_sl[2 * pp]], x[:, hg_sl[2 * pp + 1]]))
            a2 = jnp.where(hg_masks[li], a_l, a2)
        hg_a2.append(_bf(a2))
    hg_oi = [_dot_nt(q_dec[:, hg_sl[h]], _bf(hgt_ref[h])) for h in range(HG_HEADS)]
    yield

    dsel128 = dsel_ref[:, 0:128] > 0.5
    ml_s2, ml_dens = [], []
    for pp in ml_pairs:
        l0, l1 = LANE_MF + 2 * pp, LANE_MF + 2 * pp + 1
        m_row = jnp.where(lo_half, m_run[:, l0:l0 + 1], m_run[:, l1:l1 + 1])
        u_row = jnp.where(lo_half, u[:, l0:l0 + 1], u[:, l1:l1 + 1])
        u_col = jnp.sum(jnp.where(dsel128, u_row, 0.0), axis=0, keepdims=True)
        s2 = ml_s2raw[pp] * jnp.exp(jnp.where(causal2, u_col - m_row, NEG_INF))
        ml_s2.append(s2)
        ml_dens.append((jnp.sum(jnp.where(lo_half, s2, 0.0), axis=-1, keepdims=True),
                        jnp.sum(jnp.where(lo_half, 0.0, s2), axis=-1, keepdims=True)))
    ml_num2 = [_dot(_bf(ml_s2[pp]), _bf(_blockdiag_rows(ml_v[2 * pp], ml_v[2 * pp + 1]))) for pp in ml_pairs]
    yield

    y_parts = []
    for g in range(SSD_GROUPS):
        gs = slice(g * 512, (g + 1) * 512)
        for pp in range(4):
            ps = slice((g * 4 + pp) * 128, (g * 4 + pp + 1) * 128)
            seg = ac_f[:, ps] - ctr[:, ps]
            m2 = _bf(ssd_cb2[g] * jnp.exp2(jnp.where(causal2, seg, NEG_INF)))
            xp = xt[:, ps]
            xb = _bf(jnp.concatenate([jnp.where(lo_half, xp, 0.0), jnp.where(lo_half, 0.0, xp)], axis=0))
            y_parts.append(_dot(m2, xb) + eac_f[:, ps] * ssd_yi[g][:, pp * 128:(pp + 1) * 128])
        ht_ref[:, gs] = ht_ref[:, gs] * eac_end[:, gs] + _dot(_bf(ssd_bm[g].T), xdec[:, gs])
        yield

    hg_o2 = [_dot(hg_a2[pp], _blockdiag_rows(vv_b[:, hg_sl[2 * pp]], vv_b[:, hg_sl[2 * pp + 1]]))
             for pp in hg_pairs]

    for h in range(ML_HEADS):
        pp, j, ln = h // 2, h % 2, LANE_MF + h
        qh = col(C_MQ + h * 128, 128)
        wi = w_inter[:, ln:ln + 1]
        num = ml_num2[pp][:, j * 128:(j + 1) * 128] + wi * ml_qc[h]
        qn = jnp.sum(qh * mn_ref[h:h + 1, :], axis=-1, keepdims=True)
        den = ml_dens[pp][j] + wi * qn
        hval = num * (1.0 / jnp.maximum(jnp.abs(den), inv_floor[:, ln:ln + 1]))
        hn = _rms(hval, mlg_ref[:, h * 128:(h + 1) * 128])
        put(SSD_WIDTH + h * 128, _bf(col(C_MO + h * 128, 128) * hn))
        kw = ml_k[h] * wk_s[:, ln:ln + 1]
        sc = sc_s[:, ln:ln + 1]
        mc_ref[h] = sc * mc_ref[h] + _dot(_bf(kw.T), _bf(ml_v[h]))
        mn_ref[h:h + 1, :] = sc * mn_ref[h:h + 1, :] + jnp.sum(kw, axis=0, keepdims=True)
        if h % 2 == 1:
            yield
    mm_ref[...] = jnp.where(ml_lane[0:1], m_i[L - 1:L, :], 0.0)

    y = jnp.concatenate(y_parts, axis=1) + dsk_ref[...] * xs
    y_ssd = _rms(y * col(C_Z, SSD_WIDTH), sg_ref[...])
    put(0, _bf(y_ssd))
    yield

    for h in range(HG_HEADS):
        hs = hg_sl[h]
        o = hg_o2[h // 2][:, (h % 2) * 128:(h % 2 + 1) * 128] + hg_oi[h]
        on = _rms(o, hgg_ref[:, hs])
        put(SSD_WIDTH + ML_WIDTH + h * 128, _bf(on * col(C_HG + h * 128, 128)))
        hgt_ref[h] = hgt_ref[h] * s_dec[:, hs] + _dot(_bf(vv[:, hs].T), k_dec[:, hs])


def _lockstep(gens):
    gens = list(gens)
    while gens:
        alive = []
        for g in gens:
            try:
                next(g)
                alive.append(g)
            except StopIteration:
                pass
        gens = alive


def _mixer_kernel(*refs, t_rows, n_chunks, n_seq, layer, depth, has_state, has_prev):
    it = iter(refs)
    proj_ref = next(it)
    state0 = [next(it) for _ in range(6)] if has_state else None
    sp_ref, cw_ref, cb_ref, dsk_ref, sg_ref, mlg_ref, hgg_ref = (next(it) for _ in range(7))
    e3_ref, dsel_ref, causal_ref, ce3_ref, hmask_ref, isq_ref = (next(it) for _ in range(6))
    if has_prev:
        for _ in range(6):
            next(it)
    mix_ref, conv_ref, ssd_ref, mc_ref, mn_ref, mm_ref, hg_ref = (next(it) for _ in range(7))
    cbuf_ref, ht_ref, hgt_ref = (next(it) for _ in range(3))
    padded = t_rows < CHUNK
    pbuf_ref = next(it) if padded else None
    tb = pl.program_id(1)
    li = 0 if has_prev else layer
    out_refs = (conv_ref, ssd_ref, mc_ref, mn_ref, mm_ref, hg_ref)

    @pl.when(tb == 0)
    def _init():
        if not has_prev:
            for other in range(depth):
                if other != layer:
                    for r in out_refs:
                        r[other] = jnp.zeros(r.shape[1:], jnp.float32)
        for s in range(n_seq):
            if has_state:
                conv0_ref, ssd0_ref, mc0_ref, mn0_ref, mm0_ref, hg0_ref = state0
                cbuf_ref[s, 5:8, :] = conv0_ref[0, s]
                ht_ref[s] = ssd0_ref[0, s].reshape(SSD_WIDTH, SSD_STATE).T
                mc_ref[li, s] = mc0_ref[0, s]
                mn_ref[li, s] = mn0_ref[0, s]
                mm_ref[li, s] = mm0_ref[0, s]
                for h in range(HG_HEADS):
                    hgt_ref[s, h] = hg0_ref[0, s, h].T
            else:
                cbuf_ref[s, 5:8, :] = jnp.zeros((CONV_W - 1, CONV_CH), jnp.float32)
                ht_ref[s] = jnp.zeros(ht_ref.shape[1:], jnp.float32)
                mc_ref[li, s] = jnp.zeros(mc_ref.shape[2:], jnp.float32)
                mn_ref[li, s] = jnp.zeros(mn_ref.shape[2:], jnp.float32)
                mm_ref[li, s] = jnp.zeros(mm_ref.shape[2:], jnp.float32)
                hgt_ref[s] = jnp.zeros(hgt_ref.shape[1:], jnp.float32)

    def chunk(s, col, put):
        return _mixer_chunk(
            col, put, sp_ref=sp_ref, cw_ref=cw_ref, cb_ref=cb_ref, dsk_ref=dsk_ref, sg_ref=sg_ref, mlg_ref=mlg_ref,
            hgg_ref=hgg_ref, e3_ref=e3_ref, dsel_ref=dsel_ref, causal_ref=causal_ref, ce3_ref=ce3_ref,
            hmask_ref=hmask_ref, isq_ref=isq_ref, mc_ref=mc_ref.at[li, s], mn_ref=mn_ref.at[li, s],
            mm_ref=mm_ref.at[li, s], ht_ref=ht_ref.at[s], hgt_ref=hgt_ref.at[s], cbuf_ref=cbuf_ref.at[s],
            t_valid=t_rows if padded else CHUNK)

    if padded:
        gens = []
        for s in range(n_seq):
            pbuf_ref[s, 0:t_rows, :] = proj_ref[s]
            pbuf_ref[s, t_rows:CHUNK, :] = jnp.zeros((CHUNK - t_rows, PROJ_COLS), jnp.float32)

            def col(c0, width, s=s):
                return pbuf_ref[s, :, c0:c0 + width]

            def put(c0, val, s=s):
                mix_ref[s, :, c0:c0 + val.shape[1]] = val[0:t_rows]

            gens.append(chunk(s, col, put))
        _lockstep(gens)
    else:
        def body(ci, carry):
            rows = pl.ds(pl.multiple_of(ci * CHUNK, CHUNK), CHUNK)
            gens = []
            for s in range(n_seq):
                def col(c0, width, s=s):
                    return proj_ref[s, rows, c0:c0 + width]

                def put(c0, val, s=s):
                    mix_ref[s, rows, c0:c0 + val.shape[1]] = val

                gens.append(chunk(s, col, put))
            _lockstep(gens)
            return carry
        lax.fori_loop(0, n_chunks, body, 0, unroll=2 if n_chunks % 2 == 0 else 1)

    @pl.when(tb == pl.num_programs(1) - 1)
    def _fin():
        for s in range(n_seq):
            conv_ref[li, s] = cbuf_ref[s, 5:8, :]
            ssd_ref[li, s] = ht_ref[s].T.reshape(SSD_HEADS, SSD_HEADDIM, SSD_STATE)
            for h in range(HG_HEADS):
                hg_ref[li, s, h] = hgt_ref[s, h].T


def _mixer(proj, state_in, prev, params, consts, layer, depth):
    b, t, _ = proj.shape
    if t < CHUNK:
        tblk = t
    else:
        tblk = min(TIME_BLOCK, t)
        assert t % tblk == 0 and tblk % CHUNK == 0
    ns = SEQ_PER_STEP
    assert b % ns == 0
    c2 = lambda i, j: (0, 0)
    c3 = lambda i, j: (0, 0, 0)
    state_dims = [(CONV_W - 1, CONV_CH), (SSD_HEADS, SSD_HEADDIM, SSD_STATE), (ML_HEADS, ML_HD, ML_HD),
                  (ML_HEADS, ML_HD), (1, 128), (HG_HEADS, HG_HD, HG_HD)]

    def state_spec(dims, layers, first):
        return pl.BlockSpec((layers, ns) + dims, lambda i, j: (first, i) + (0,) * len(dims))

    in_state_specs = [state_spec(d, 1, layer) for d in state_dims]
    out_state_specs = [state_spec(d, 1, layer) if prev is not None else state_spec(d, depth, 0) for d in state_dims]
    state_shapes = [
        jax.ShapeDtypeStruct((depth, b, CONV_W - 1, CONV_CH), jnp.float32),
        jax.ShapeDtypeStruct((depth, b, SSD_HEADS, SSD_HEADDIM, SSD_STATE), jnp.float32),
        jax.ShapeDtypeStruct((depth, b, ML_HEADS, ML_HD, ML_HD), jnp.float32),
        jax.ShapeDtypeStruct((depth, b, ML_HEADS, ML_HD), jnp.float32),
        jax.ShapeDtypeStruct((depth, b, 1, 128), jnp.float32),
        jax.ShapeDtypeStruct((depth, b, HG_HEADS, HG_HD, HG_HD), jnp.float32),
    ]
    args = [proj]
    in_specs = [pl.BlockSpec((ns, tblk, PROJ_COLS), lambda i, j: (i, j, 0))]
    if state_in is not None:
        args += list(state_in)
        in_specs += in_state_specs
    args += list(params) + list(consts)
    in_specs += [pl.BlockSpec(a.shape, c3 if a.ndim == 3 else c2) for a in list(params) + list(consts)]
    aliases = {}
    if prev is not None:
        for k, a in enumerate(prev):
            aliases[len(args)] = 1 + k
            args.append(a)
            in_specs.append(pl.BlockSpec(memory_space=pl.ANY))
    scratch = [pltpu.VMEM((ns, 8 + CHUNK, CONV_CH), jnp.float32),
               pltpu.VMEM((ns, SSD_STATE, SSD_WIDTH), jnp.float32),
               pltpu.VMEM((ns, HG_HEADS, HG_HD, HG_HD), jnp.float32)]
    if t < CHUNK:
        scratch.append(pltpu.VMEM((ns, CHUNK, PROJ_COLS), jnp.float32))
    return pl.pallas_call(
        functools.partial(_mixer_kernel, t_rows=tblk, n_chunks=max(tblk // CHUNK, 1), n_seq=ns, layer=layer,
                          depth=depth, has_state=state_in is not None, has_prev=prev is not None),
        grid=(b // ns, t // tblk),
        in_specs=in_specs,
        out_specs=[pl.BlockSpec((ns, tblk, D_MIX), lambda i, j: (i, j, 0))] + out_state_specs,
        out_shape=[jax.ShapeDtypeStruct((b, t, D_MIX), jnp.bfloat16)] + state_shapes,
        scratch_shapes=scratch,
        input_output_aliases=aliases,
        compiler_params=pltpu.CompilerParams(
            dimension_semantics=("parallel", "arbitrary"), vmem_limit_bytes=VMEM_LIMIT),
        name="mixer",
    )(*args)


_W_IN_GROUPS = (
    (1024, C_Z), (1536, C_XBC), (16, C_SM + LANE_DT), (512, C_MQ), (512, C_MK), (512, C_MV),
    (4, C_SM + LANE_MI), (4, C_SM + LANE_MF), (512, C_MO), (512, C_HQ), (512, C_HK), (512, C_HI), (512, C_HG),
)
W_IN_COLS = sum(width for width, _ in _W_IN_GROUPS)
REGROUP_ROWS = 128


def _regroup_kernel(w_ref, o_ref):
    o_ref[:, C_SM:C_SM + 128] = jnp.zeros((REGROUP_ROWS, 128), jnp.bfloat16)
    src = 0
    for width, dst in _W_IN_GROUPS:
        o_ref[:, dst:dst + width] = w_ref[:, src:src + width].astype(jnp.bfloat16)
        src += width


def _regroup_w_in(w):
    depth = w.shape[0]
    return pl.pallas_call(
        _regroup_kernel,
        grid=(depth, D_MODEL // REGROUP_ROWS),
        in_specs=[pl.BlockSpec((None, REGROUP_ROWS, W_IN_COLS), lambda l, i: (l, i, 0))],
        out_specs=pl.BlockSpec((None, REGROUP_ROWS, W_COLS), lambda l, i: (l, i, 0)),
        out_shape=jax.ShapeDtypeStruct((depth, D_MODEL, W_COLS), jnp.bfloat16),
        compiler_params=pltpu.CompilerParams(
            dimension_semantics=("parallel", "parallel"), vmem_limit_bytes=VMEM_LIMIT),
        name="regroup_w_in",
    )(w)


def _small_params(dt_bias, a_log, ml_bi, ml_bf):
    sp = jnp.zeros((8, 128), jnp.float32)
    sp = sp.at[0, LANE_DT:LANE_DT + SSD_HEADS].set(dt_bias)
    sp = sp.at[1, LANE_DT:LANE_DT + SSD_HEADS].set(a_log)
    sp = sp.at[2, LANE_MF:LANE_MF + ML_HEADS].set(ml_bi)
    sp = sp.at[3, LANE_MF:LANE_MF + ML_HEADS].set(ml_bf)
    return sp


def _run_trunk(x, state_in, lp, norm_final, consts, tm_proj, tm_ffn):
    b, t, _ = x.shape
    assert t % CHUNK == 0 or t < CHUNK, "a partial chunk is only supported for single-chunk sequences"
    depth = len(lp)
    x2d = x.reshape(b * t, D_MODEL)
    states = None
    for l in range(depth):
        p = lp[l]
        params = (p["sp"], p["conv_w"], p["conv_b"], p["d_skip"], p["ssd_gain"], p["ml_gain"], p["hg_gain"])
        proj = _in_proj(x2d, p["norm_mix"], p["w_in"], p["hg_lb"], tm_proj, l)
        res = _mixer(proj.reshape(b, t, PROJ_COLS), state_in, states, params, consts, l, depth)
        states = res[1:]
        x2d = _out_ffn(x2d, res[0].reshape(b * t, D_MIX), p["w_out"], p["norm_ffn"], p["w_ffn_in"],
                       p["w_ffn_out"], norm_final, tm_ffn, l, l == depth - 1)
    conv, ssd, mc, mn, mm_p, hg = states
    return x2d.reshape(b, t, D_MODEL), (conv, ssd, mc, mn, mm_p[:, :, 0, LANE_MF:LANE_MF + ML_HEADS], hg)


def kernel(x_prompt, x_sample, state_conv, state_ssd, state_mlstm_c, state_mlstm_n, state_mlstm_m, state_hgrn,
           norm_mix, w_in, conv_w, conv_b, dt_bias, a_log, d_skip, ssd_gain, ml_bi, ml_bf, ml_gain,
           hg_lb, hg_gain, w_out, norm_ffn, w_ffn_in, w_ffn_out, norm_final):
    depth = w_in.shape[0]
    f32 = jnp.float32
    consts = _mixer_constants()
    bf16 = jnp.bfloat16
    w_in_b, w_out_b, w_fi_b, w_fo_b = _regroup_w_in(w_in), w_out.astype(bf16), w_ffn_in.astype(bf16), w_ffn_out.astype(bf16)
    lp = []
    for l in range(depth):
        lp.append({
            "norm_mix": norm_mix[l].reshape(1, D_MODEL),
            "w_in": w_in_b,
            "sp": _small_params(dt_bias[l], a_log[l], ml_bi[l], ml_bf[l]),
            "conv_w": conv_w[l],
            "conv_b": conv_b[l].reshape(1, CONV_CH),
            "d_skip": jnp.repeat(d_skip[l], SSD_HEADDIM).reshape(1, SSD_WIDTH),
            "ssd_gain": ssd_gain[l].reshape(1, SSD_WIDTH),
            "ml_gain": ml_gain[l].reshape(1, ML_WIDTH),
            "hg_lb": hg_lb.astype(f32),
            "hg_gain": hg_gain[l].reshape(1, HG_WIDTH),
            "w_out": w_out_b,
            "norm_ffn": norm_ffn[l].reshape(1, D_MODEL),
            "w_ffn_in": w_fi_b,
            "w_ffn_out": w_fo_b,
        })
    nf = norm_final.reshape(1, D_MODEL)

    y_prompt, p_outs = _run_trunk(x_prompt, None, lp, nf, consts, 256, 512)
    bs = x_sample.shape[0]
    mm_p = jnp.pad(state_mlstm_m.astype(f32)[:, :, None, :],
                   ((0, 0), (0, 0), (0, 0), (LANE_MF, 128 - LANE_MF - ML_HEADS)))
    s_in = (state_conv, state_ssd, state_mlstm_c, state_mlstm_n, mm_p, state_hgrn)
    y_sample, s_outs = _run_trunk(x_sample, s_in, lp, nf, consts, 256, 512)
    return (y_prompt, y_sample) + p_outs + s_outs
```

```python
import functools

import numpy as np
import jax
import jax.numpy as jnp
from jax import lax
from jax.experimental import pallas as pl
from jax.experimental.pallas import tpu as pltpu

D_MODEL = 1024
CHUNK = 64
SSD_WIDTH = 1024
SSD_HEADDIM = 64
SSD_HEADS = 16
SSD_GROUPS = 2
SSD_STATE = 128
CONV_W = 4
CONV_CH = SSD_WIDTH + 2 * SSD_GROUPS * SSD_STATE
ML_WIDTH = 512
ML_HEADS = 4
ML_HD = 128
HG_WIDTH = 512
HG_HEADS = 4
HG_HD = 128
D_MIX = 2048
D_FF = 2816
EPS = 1e-6

C_Z = 0
C_XBC = 1024
C_MQ, C_MK, C_MV, C_MO = 2560, 3072, 3584, 4096
C_HQ, C_HK, C_HI, C_HG = 4608, 5120, 5632, 6144
C_SM = 6656
W_COLS = 6784
C_HL = 6784
PROJ_COLS = 7296
TIME_BLOCK = 256
SEQ_PER_STEP = 1
LOG2E = 1.4426950408889634
LANE_DT = 0
LANE_MI = 16
LANE_MF = 20

VMEM_LIMIT = 52 * 1024 * 1024
SUBLANES = 8
CONV_TAIL = SUBLANES - (CONV_W - 1)
NEG_INF = float("-inf")

HG_LEVELS = (32, 16, 8, 4, 2, 1)

_NT = (((1,), (1,)), ((), ()))


def _bf(x):
    return x.astype(jnp.bfloat16)


def _dot(a, b):
    return jnp.dot(a, b, preferred_element_type=jnp.float32)


def _dot_nt(a, b):
    return lax.dot_general(a, b, _NT, preferred_element_type=jnp.float32)


def _sigmoid(x):
    return 1.0 / (1.0 + jnp.exp(-x))


def _silu(x):
    return x * _sigmoid(x)


def _softplus(x):
    return jnp.maximum(x, 0.0) + jnp.log(1.0 + jnp.exp(-jnp.abs(x)))


def _split3(x):
    hi = _bf(x)
    r1 = x - hi.astype(jnp.float32)
    mid = _bf(r1)
    r2 = r1 - mid.astype(jnp.float32)
    return hi, mid, _bf(r2)


def _rms(x, gain):
    return x * lax.rsqrt(jnp.mean(x * x, axis=-1, keepdims=True) + EPS) * gain


def _hg_lower_bound(lb_all, layer):
    lb_e = jnp.exp(lb_all - jnp.max(lb_all, axis=0, keepdims=True))
    lb_soft = lb_e * (1.0 / jnp.sum(lb_e, axis=0, keepdims=True))
    return jnp.sum(lb_soft[0:layer + 1], axis=0, keepdims=True) - lb_soft[0:1]


def _causal_conv(cbuf_ref, cw_ref, cb_ref, rows):
    acc = cb_ref[...] + cbuf_ref[SUBLANES:SUBLANES + rows, :] * cw_ref[CONV_W - 1:CONV_W, :]
    for w in range(CONV_W - 1):
        acc = acc + cbuf_ref[CONV_TAIL + w:CONV_TAIL + w + rows, :] * cw_ref[w:w + 1, :]
    return acc


_PROJ_GROUPS = (
    (C_Z, 512, "silu"), (C_Z + 512, 512, "silu"),
    (C_XBC, 512, None), (C_XBC + 512, 512, None), (C_XBC + 1024, 512, None),
    (C_MQ, 512, None), (C_MK, 512, "kscale"), (C_MV, 512, None), (C_MO, 512, "sigmoid"),
    (C_HQ, 512, "silu"), (C_HK, 512, "fgate"), (C_HI, 512, None), (C_HG, 512, "silu"),
    (C_SM, 128, None),
)


def _proj_group(h, w_ref, lb_ref, layer, group, store):
    c0, width, act = group
    acc = _dot(h, w_ref[:, c0:c0 + width])
    if act == "silu":
        acc = _silu(acc)
    elif act == "sigmoid":
        acc = _sigmoid(acc)
    elif act == "kscale":
        acc = acc * (ML_HD ** -0.5)
    elif act == "fgate":
        lb = _hg_lower_bound(lb_ref[...], layer)
        fg = lb + (1.0 - lb) * _sigmoid(acc)
        store(C_HL, jnp.log2(fg))
        acc = 1.0 - fg
    store(c0, acc)


def _in_proj_kernel(x_ref, g_ref, w_ref, lb_ref, o_ref, *, layer):
    h = _bf(_rms(x_ref[...], g_ref[...]))

    def store(c0, val):
        o_ref[:, c0:c0 + val.shape[1]] = val

    for group in _PROJ_GROUPS:
        _proj_group(h, w_ref, lb_ref, layer, group, store)


def _in_proj(x2d, gain, w, hg_lb, tm, layer):
    n = x2d.shape[0]
    c2 = lambda i: (0, 0)
    return pl.pallas_call(
        functools.partial(_in_proj_kernel, layer=layer),
        grid=(n // tm,),
        in_specs=[
            pl.BlockSpec((tm, D_MODEL), lambda i: (i, 0)),
            pl.BlockSpec((1, D_MODEL), c2),
            pl.BlockSpec((None, D_MODEL, W_COLS), lambda i: (layer, 0, 0), pipeline_mode=pl.Buffered(1)),
            pl.BlockSpec(hg_lb.shape, c2),
        ],
        out_specs=pl.BlockSpec((tm, PROJ_COLS), lambda i: (i, 0)),
        out_shape=jax.ShapeDtypeStruct((n, PROJ_COLS), jnp.float32),
        compiler_params=pltpu.CompilerParams(
            dimension_semantics=("parallel",), vmem_limit_bytes=VMEM_LIMIT),
        name="in_proj",
    )(x2d, gain, w, hg_lb)


FF_TILE = 256


def _out_ffn_kernel(x_ref, mix_ref, wo_ref, gn_ref, wi_ref, wf_ref, gf_ref, o_ref, *, final_norm):
    x1 = x_ref[...] + _dot(mix_ref[...], wo_ref[...])
    h2 = _bf(_rms(x1, gn_ref[...]))
    acc = x1
    for j in range(D_FF // FF_TILE):
        g = _dot(h2, wi_ref[:, j * FF_TILE:(j + 1) * FF_TILE])
        u = _dot(h2, wi_ref[:, D_FF + j * FF_TILE:D_FF + (j + 1) * FF_TILE])
        acc = acc + _dot(_bf(_silu(g) * u), wf_ref[j * FF_TILE:(j + 1) * FF_TILE, :])
    if final_norm:
        acc = _rms(acc, gf_ref[...])
    o_ref[...] = acc


def _out_ffn(x2d, mix2d, w_out, g_ffn, w_fi, w_fo, g_final, tm, layer, final_norm):
    n = x2d.shape[0]
    const = lambda i: (0, 0)
    lyr = lambda i: (layer, 0, 0)
    one = pl.Buffered(1)
    return pl.pallas_call(
        functools.partial(_out_ffn_kernel, final_norm=final_norm),
        grid=(n // tm,),
        in_specs=[
            pl.BlockSpec((tm, D_MODEL), lambda i: (i, 0)),
            pl.BlockSpec((tm, D_MIX), lambda i: (i, 0)),
            pl.BlockSpec((None, D_MIX, D_MODEL), lyr, pipeline_mode=one),
            pl.BlockSpec((1, D_MODEL), const),
            pl.BlockSpec((None, D_MODEL, 2 * D_FF), lyr, pipeline_mode=one),
            pl.BlockSpec((None, D_FF, D_MODEL), lyr, pipeline_mode=one),
            pl.BlockSpec((1, D_MODEL), const),
        ],
        out_specs=pl.BlockSpec((tm, D_MODEL), lambda i: (i, 0)),
        out_shape=jax.ShapeDtypeStruct((n, D_MODEL), jnp.float32),
        compiler_params=pltpu.CompilerParams(
            dimension_semantics=("parallel",), vmem_limit_bytes=VMEM_LIMIT),
        name="out_ffn",
    )(x2d, mix2d, w_out, g_ffn, w_fi, w_fo, g_final)


def _mixer_constants():
    L = CHUNK
    e3 = np.zeros((128, SSD_WIDTH), np.float32)
    for piece in range(3):
        for h in range(SSD_HEADS):
            e3[piece * SSD_HEADS + h, h * SSD_HEADDIM:(h + 1) * SSD_HEADDIM] = 1.0

    l = np.arange(L)[:, None]
    c = np.arange(SSD_WIDTH)[None, :]
    dsel = ((c % L) == l).astype(np.float32)
    s2 = np.arange(128)[None, :] % L
    causal2 = (s2 <= l).astype(np.float32)

    t = np.arange(L)
    blocks = []
    masks = []
    isq = np.zeros((L, 128), np.float32)
    for li, m in enumerate(HG_LEVELS):
        cm = np.zeros((L, L), np.float32)
        start = (t // (2 * m)) * (2 * m)
        mid = start + m - 1
        query = (t - start) >= m
        for s in range(L):
            if query[s]:
                cm[s, mid[s] + 1:s + 1] = 1.0
            else:
                cm[s, s + 1:mid[s] + 1] = 1.0
        blocks.append(cm)
        same = (start[:, None] == start[None, :])
        mk = same & query[:, None] & (~query[None, :])
        masks.append(np.concatenate([mk, mk], axis=1).astype(np.float32))
        isq[:, li] = query.astype(np.float32)
    tri = (t[None, :] <= t[:, None]).astype(np.float32)
    blocks.append(tri)
    blocks.append((t[None, :] > t[:, None]).astype(np.float32))
    ce = np.concatenate(blocks, axis=0)
    ce3 = np.concatenate([ce, ce, ce], axis=1)
    diag = (t[:, None] == t[None, :])
    masks.append(np.concatenate([diag, diag], axis=1).astype(np.float32))
    hmask = np.stack(masks, axis=0)
    return (jnp.asarray(e3, jnp.bfloat16), jnp.asarray(dsel), jnp.asarray(causal2),
            jnp.asarray(ce3, jnp.bfloat16), jnp.asarray(hmask), jnp.asarray(isq))


def _scan_time(x, op, identity, row):
    sh = 1
    while sh < x.shape[0]:
        r = pltpu.roll(x, sh, axis=0)
        x = op(x, jnp.where(row >= sh, r, identity))
        sh *= 2
    return x


def _blockdiag_rows(a, b):
    z = jnp.zeros_like(a)
    return jnp.concatenate([jnp.concatenate([a, z], axis=1), jnp.concatenate([z, b], axis=1)], axis=0)


def _mixer_chunk(col, put, sp_ref, cw_ref, cb_ref, dsk_ref, sg_ref, mlg_ref, hgg_ref,
                 e3_ref, dsel_ref, causal_ref, ce3_ref, hmask_ref, isq_ref,
                 mc_ref, mn_ref, mm_ref, ht_ref, hgt_ref, cbuf_ref, *, t_valid):
    L = CHUNK
    row = lax.broadcasted_iota(jnp.int32, (L, 128), 0)
    lane = lax.broadcasted_iota(jnp.int32, (L, 128), 1)
    lo_half = lane < 64
    causal2 = causal_ref[...] > 0.5

    nl = len(HG_LEVELS)
    ml_pairs = range(ML_HEADS // 2)
    hg_pairs = range(HG_HEADS // 2)

    sm = col(C_SM, 128)
    dt = _softplus(sm + sp_ref[0:1, :])
    a_neg2 = jnp.where(lane[0:1] < SSD_HEADS, -LOG2E * jnp.exp(sp_ref[1:2, :]), 0.0)
    ig = pltpu.roll(sm, LANE_MF - LANE_MI, axis=1) + sp_ref[2:3, :]
    lf = -_softplus(-(sm + sp_ref[3:4, :]))
    if t_valid < L:
        ok = row < t_valid
        dt = jnp.where(ok, dt, 0.0)
        ig = jnp.where(ok, ig, NEG_INF)
        lf = jnp.where(ok, lf, 0.0)
    ml_lane = (lane >= LANE_MF) & (lane < LANE_MF + ML_HEADS)
    cs = _scan_time(jnp.where(lane < SSD_HEADS, dt * a_neg2, jnp.where(ml_lane, lf, 0.0)), jnp.add, 0.0, row)
    acum2 = cs
    u = jnp.where(ml_lane, ig - cs, 0.0)
    m_prev = mm_ref[...]
    m_run = jnp.maximum(_scan_time(u, jnp.maximum, NEG_INF, row), m_prev)
    m_i = cs + m_run
    m_last = m_run[L - 1:L, :]
    w_inter = jnp.exp(m_prev - m_run)
    inv_floor = jnp.exp(-m_i)
    wk_s = jnp.exp(u - m_last)
    sc_s = jnp.exp(m_prev - m_last)

    hi, mid, lo = _split3(jnp.concatenate([dt, acum2], axis=0))
    lane2 = lax.broadcasted_iota(jnp.int32, (2 * L, 128), 1)
    packed = jnp.where(
        lane2 < SSD_HEADS, hi.astype(jnp.float32),
        jnp.where(lane2 < 2 * SSD_HEADS, pltpu.roll(mid.astype(jnp.float32), SSD_HEADS, axis=1),
                  jnp.where(lane2 < 3 * SSD_HEADS, pltpu.roll(lo.astype(jnp.float32), 2 * SSD_HEADS, axis=1), 0.0)))
    ex = _dot(_bf(packed), e3_ref[...])
    glog2 = col(C_HL, HG_WIDTH)
    if t_valid < L:
        glog2 = jnp.where(lax.broadcasted_iota(jnp.int32, (L, HG_WIDTH), 0) < t_valid, glog2, 0.0)
    ghi, gmid, glo = _split3(glog2)
    exps = _dot(ce3_ref[...], jnp.concatenate([ghi, gmid, glo], axis=0))
    yield

    xbc = col(C_XBC, CONV_CH)
    cbuf_ref[SUBLANES:SUBLANES + L, :] = xbc
    xbc_a = _silu(_causal_conv(cbuf_ref, cw_ref, cb_ref, L))
    cbuf_ref[CONV_TAIL:SUBLANES, :] = xbc[t_valid - (CONV_W - 1):t_valid, :]
    xs = xbc_a[:, 0:SSD_WIDTH]
    yield

    ml_q2 = [_bf(col(C_MQ + pp * 256, 256)) for pp in ml_pairs]
    ml_k = [col(C_MK + h * 128, 128) for h in range(ML_HEADS)]
    ml_v = [col(C_MV + h * 128, 128) for h in range(ML_HEADS)]
    ml_s2raw = [_dot_nt(ml_q2[pp], _bf(_blockdiag_rows(ml_k[2 * pp], ml_k[2 * pp + 1]))) for pp in ml_pairs]
    ml_qc = [_dot(ml_q2[h // 2][:, (h % 2) * 128:(h % 2 + 1) * 128], _bf(mc_ref[h])) for h in range(ML_HEADS)]

    kk = col(C_HK, HG_WIDTH)
    qq = col(C_HQ, HG_WIDTH)
    vv = col(C_HI, HG_WIDTH)
    qq_b, kk_b, vv_b = _bf(qq), _bf(kk), _bf(vv)
    hg_sl = [slice(h * 128, (h + 1) * 128) for h in range(HG_HEADS)]
    hg_adiag = [_dot_nt(qq_b[:, pp * 256:(pp + 1) * 256],
                        _blockdiag_rows(kk_b[:, hg_sl[2 * pp]], kk_b[:, hg_sl[2 * pp + 1]])) for pp in hg_pairs]

    ssd_bm, ssd_cb2, ssd_yi = [], [], []
    for g in range(SSD_GROUPS):
        bm = xbc_a[:, SSD_WIDTH + g * SSD_STATE:SSD_WIDTH + (g + 1) * SSD_STATE]
        cm = xbc_a[:, SSD_WIDTH + (SSD_GROUPS + g) * SSD_STATE:SSD_WIDTH + (SSD_GROUPS + g + 1) * SSD_STATE]
        bm_b, cm_b = _bf(bm), _bf(cm)
        ssd_bm.append(bm)
        ssd_cb2.append(_dot_nt(cm_b, jnp.concatenate([bm_b, bm_b], axis=0)))
        ssd_yi.append(_dot(cm_b, _bf(ht_ref[:, g * 512:(g + 1) * 512])))
    yield

    dt_f, ac_f = ex[0:L], ex[L:2 * L]
    eac_f = jnp.exp2(ac_f)
    dend_f = jnp.exp2(ac_f[L - 1:L, :] - ac_f)
    xt = xs * dt_f
    ctr = jnp.sum(jnp.where(dsel_ref[...] > 0.5, ac_f, 0.0), axis=0, keepdims=True)
    xdec = _bf(xt * dend_f)
    eac_end = eac_f[L - 1:L, :]
    yield

    qk_lv = []
    for li in range(nl):
        isq = isq_ref[:, li:li + 1] > 0.5
        qk_lv.append(_bf(jnp.where(isq, qq, kk) * jnp.exp2(exps[li * L:(li + 1) * L])))
    gc = exps[nl * L:(nl + 1) * L]
    gend = exps[(nl + 1) * L:(nl + 2) * L]
    q_dec = _bf(qq * jnp.exp2(gc))
    k_dec = _bf(kk * jnp.exp2(gend))
    s_dec = jnp.exp2(gc[L - 1:L, :])
    yield

    hg_masks = [hmask_ref[li] > 0.5 for li in range(nl + 1)]
    hg_a2 = []
    for pp in hg_pairs:
        a2 = jnp.where(hg_masks[nl], hg_adiag[pp], 0.0)
        for li in range(nl):
            x = qk_lv[li]
            a_l = _dot_nt(x[:, pp * 256:(pp + 1) * 256], _blockdiag_rows(x[:, hg_sl[2 * pp]], x[:, hg_sl[2 * pp + 1]]))
            a2 = jnp.where(hg_masks[li], a_l, a2)
        hg_a2.append(_bf(a2))
    hg_oi = [_dot_nt(q_dec[:, hg_sl[h]], _bf(hgt_ref[h])) for h in range(HG_HEADS)]
    yield

    dsel128 = dsel_ref[:, 0:128] > 0.5
    ml_s2, ml_dens = [], []
    for pp in ml_pairs:
        l0, l1 = LANE_MF + 2 * pp, LANE_MF + 2 * pp + 1
        m_row = jnp.where(lo_half, m_run[:, l0:l0 + 1], m_run[:, l1:l1 + 1])
        u_row = jnp.where(lo_half, u[:, l0:l0 + 1], u[:, l1:l1 + 1])
        u_col = jnp.sum(jnp.where(dsel128, u_row, 0.0), axis=0, keepdims=True)
        s2 = ml_s2raw[pp] * jnp.exp(jnp.where(causal2, u_col - m_row, NEG_INF))
        ml_s2.append(s2)
        ml_dens.append((jnp.sum(jnp.where(lo_half, s2, 0.0), axis=-1, keepdims=True),
                        jnp.sum(jnp.where(lo_half, 0.0, s2), axis=-1, keepdims=True)))
    ml_num2 = [_dot(_bf(ml_s2[pp]), _bf(_blockdiag_rows(ml_v[2 * pp], ml_v[2 * pp + 1]))) for pp in ml_pairs]
    yield

    y_parts = []
    for g in range(SSD_GROUPS):
        gs = slice(g * 512, (g + 1) * 512)
        for pp in range(4):
            ps = slice((g * 4 + pp) * 128, (g * 4 + pp + 1) * 128)
            seg = ac_f[:, ps] - ctr[:, ps]
            m2 = _bf(ssd_cb2[g] * jnp.exp2(jnp.where(causal2, seg, NEG_INF)))
            xp = xt[:, ps]
            xb = _bf(jnp.concatenate([jnp.where(lo_half, xp, 0.0), jnp.where(lo_half, 0.0, xp)], axis=0))
            y_parts.append(_dot(m2, xb) + eac_f[:, ps] * ssd_yi[g][:, pp * 128:(pp + 1) * 128])
        ht_ref[:, gs] = ht_ref[:, gs] * eac_end[:, gs] + _dot(_bf(ssd_bm[g].T), xdec[:, gs])
        yield

    hg_o2 = [_dot(hg_a2[pp], _blockdiag_rows(vv_b[:, hg_sl[2 * pp]], vv_b[:, hg_sl[2 * pp + 1]]))
             for pp in hg_pairs]

    for h in range(ML_HEADS):
        pp, j, ln = h // 2, h % 2, LANE_MF + h
        qh = col(C_MQ + h * 128, 128)
        wi = w_inter[:, ln:ln + 1]
        num = ml_num2[pp][:, j * 128:(j + 1) * 128] + wi * ml_qc[h]
        qn = jnp.sum(qh * mn_ref[h:h + 1, :], axis=-1, keepdims=True)
        den = ml_dens[pp][j] + wi * qn
        hval = num * (1.0 / jnp.maximum(jnp.abs(den), inv_floor[:, ln:ln + 1]))
        hn = _rms(hval, mlg_ref[:, h * 128:(h + 1) * 128])
        put(SSD_WIDTH + h * 128, _bf(col(C_MO + h * 128, 128) * hn))
        kw = ml_k[h] * wk_s[:, ln:ln + 1]
        sc = sc_s[:, ln:ln + 1]
        mc_ref[h] = sc * mc_ref[h] + _dot(_bf(kw.T), _bf(ml_v[h]))
        mn_ref[h:h + 1, :] = sc * mn_ref[h:h + 1, :] + jnp.sum(kw, axis=0, keepdims=True)
        if h % 2 == 1:
            yield
    mm_ref[...] = jnp.where(ml_lane[0:1], m_i[L - 1:L, :], 0.0)

    y = jnp.concatenate(y_parts, axis=1) + dsk_ref[...] * xs
    y_ssd = _rms(y * col(C_Z, SSD_WIDTH), sg_ref[...])
    put(0, _bf(y_ssd))
    yield

    for h in range(HG_HEADS):
        hs = hg_sl[h]
        o = hg_o2[h // 2][:, (h % 2) * 128:(h % 2 + 1) * 128] + hg_oi[h]
        on = _rms(o, hgg_ref[:, hs])
        put(SSD_WIDTH + ML_WIDTH + h * 128, _bf(on * col(C_HG + h * 128, 128)))
        hgt_ref[h] = hgt_ref[h] * s_dec[:, hs] + _dot(_bf(vv[:, hs].T), k_dec[:, hs])


def _lockstep(gens):
    gens = list(gens)
    while gens:
        alive = []
        for g in gens:
            try:
                next(g)
                alive.append(g)
            except StopIteration:
                pass
        gens = alive


def _mixer_kernel(*refs, t_rows, n_chunks, n_seq, layer, depth, has_state, has_prev):
    it = iter(refs)
    proj_ref = next(it)
    state0 = [next(it) for _ in range(6)] if has_state else None
    sp_ref, cw_ref, cb_ref, dsk_ref, sg_ref, mlg_ref, hgg_ref = (next(it) for _ in range(7))
    e3_ref, dsel_ref, causal_ref, ce3_ref, hmask_ref, isq_ref = (next(it) for _ in range(6))
    if has_prev:
        for _ in range(6):
            next(it)
    mix_ref, conv_ref, ssd_ref, mc_ref, mn_ref, mm_ref, hg_ref = (next(it) for _ in range(7))
    cbuf_ref, ht_ref, hgt_ref = (next(it) for _ in range(3))
    padded = t_rows < CHUNK
    pbuf_ref = next(it) if padded else None
    tb = pl.program_id(1)
    li = 0 if has_prev else layer
    out_refs = (conv_ref, ssd_ref, mc_ref, mn_ref, mm_ref, hg_ref)

    @pl.when(tb == 0)
    def _init():
        if not has_prev:
            for other in range(depth):
                if other != layer:
                    for r in out_refs:
                        r[other] = jnp.zeros(r.shape[1:], jnp.float32)
        for s in range(n_seq):
            if has_state:
                conv0_ref, ssd0_ref, mc0_ref, mn0_ref, mm0_ref, hg0_ref = state0
                cbuf_ref[s, CONV_TAIL:SUBLANES, :] = conv0_ref[0, s]
                ht_ref[s] = ssd0_ref[0, s].reshape(SSD_WIDTH, SSD_STATE).T
                mc_ref[li, s] = mc0_ref[0, s]
                mn_ref[li, s] = mn0_ref[0, s]
                mm_ref[li, s] = mm0_ref[0, s]
                for h in range(HG_HEADS):
                    hgt_ref[s, h] = hg0_ref[0, s, h].T
            else:
                cbuf_ref[s, CONV_TAIL:SUBLANES, :] = jnp.zeros((CONV_W - 1, CONV_CH), jnp.float32)
                ht_ref[s] = jnp.zeros(ht_ref.shape[1:], jnp.float32)
                mc_ref[li, s] = jnp.zeros(mc_ref.shape[2:], jnp.float32)
                mn_ref[li, s] = jnp.zeros(mn_ref.shape[2:], jnp.float32)
                mm_ref[li, s] = jnp.zeros(mm_ref.shape[2:], jnp.float32)
                hgt_ref[s] = jnp.zeros(hgt_ref.shape[1:], jnp.float32)

    def chunk(s, col, put):
        return _mixer_chunk(
            col, put, sp_ref=sp_ref, cw_ref=cw_ref, cb_ref=cb_ref, dsk_ref=dsk_ref, sg_ref=sg_ref, mlg_ref=mlg_ref,
            hgg_ref=hgg_ref, e3_ref=e3_ref, dsel_ref=dsel_ref, causal_ref=causal_ref, ce3_ref=ce3_ref,
            hmask_ref=hmask_ref, isq_ref=isq_ref, mc_ref=mc_ref.at[li, s], mn_ref=mn_ref.at[li, s],
            mm_ref=mm_ref.at[li, s], ht_ref=ht_ref.at[s], hgt_ref=hgt_ref.at[s], cbuf_ref=cbuf_ref.at[s],
            t_valid=t_rows if padded else CHUNK)

    if padded:
        gens = []
        for s in range(n_seq):
            pbuf_ref[s, 0:t_rows, :] = proj_ref[s]
            pbuf_ref[s, t_rows:CHUNK, :] = jnp.zeros((CHUNK - t_rows, PROJ_COLS), jnp.float32)

            def col(c0, width, s=s):
                return pbuf_ref[s, :, c0:c0 + width]

            def put(c0, val, s=s):
                mix_ref[s, :, c0:c0 + val.shape[1]] = val[0:t_rows]

            gens.append(chunk(s, col, put))
        _lockstep(gens)
    else:
        def body(ci, carry):
            rows = pl.ds(pl.multiple_of(ci * CHUNK, CHUNK), CHUNK)
            gens = []
            for s in range(n_seq):
                def col(c0, width, s=s):
                    return proj_ref[s, rows, c0:c0 + width]

                def put(c0, val, s=s):
                    mix_ref[s, rows, c0:c0 + val.shape[1]] = val

                gens.append(chunk(s, col, put))
            _lockstep(gens)
            return carry
        lax.fori_loop(0, n_chunks, body, 0, unroll=2 if n_chunks % 2 == 0 else 1)

    @pl.when(tb == pl.num_programs(1) - 1)
    def _fin():
        for s in range(n_seq):
            conv_ref[li, s] = cbuf_ref[s, CONV_TAIL:SUBLANES, :]
            ssd_ref[li, s] = ht_ref[s].T.reshape(SSD_HEADS, SSD_HEADDIM, SSD_STATE)
            for h in range(HG_HEADS):
                hg_ref[li, s, h] = hgt_ref[s, h].T


def _mixer(proj, state_in, prev, params, consts, layer, depth):
    b, t, _ = proj.shape
    if t < CHUNK:
        tblk = t
    else:
        tblk = min(TIME_BLOCK, t)
        assert t % tblk == 0 and tblk % CHUNK == 0
    ns = SEQ_PER_STEP
    assert b % ns == 0
    c2 = lambda i, j: (0, 0)
    c3 = lambda i, j: (0, 0, 0)
    state_dims = [(CONV_W - 1, CONV_CH), (SSD_HEADS, SSD_HEADDIM, SSD_STATE), (ML_HEADS, ML_HD, ML_HD),
                  (ML_HEADS, ML_HD), (1, 128), (HG_HEADS, HG_HD, HG_HD)]

    def state_spec(dims, layers, first):
        return pl.BlockSpec((layers, ns) + dims, lambda i, j: (first, i) + (0,) * len(dims))

    in_state_specs = [state_spec(d, 1, layer) for d in state_dims]
    out_state_specs = [state_spec(d, 1, layer) if prev is not None else state_spec(d, depth, 0) for d in state_dims]
    state_shapes = [
        jax.ShapeDtypeStruct((depth, b, CONV_W - 1, CONV_CH), jnp.float32),
        jax.ShapeDtypeStruct((depth, b, SSD_HEADS, SSD_HEADDIM, SSD_STATE), jnp.float32),
        jax.ShapeDtypeStruct((depth, b, ML_HEADS, ML_HD, ML_HD), jnp.float32),
        jax.ShapeDtypeStruct((depth, b, ML_HEADS, ML_HD), jnp.float32),
        jax.ShapeDtypeStruct((depth, b, 1, 128), jnp.float32),
        jax.ShapeDtypeStruct((depth, b, HG_HEADS, HG_HD, HG_HD), jnp.float32),
    ]
    args = [proj]
    in_specs = [pl.BlockSpec((ns, tblk, PROJ_COLS), lambda i, j: (i, j, 0))]
    if state_in is not None:
        args += list(state_in)
        in_specs += in_state_specs
    args += list(params) + list(consts)
    in_specs += [pl.BlockSpec(a.shape, c3 if a.ndim == 3 else c2) for a in list(params) + list(consts)]
    aliases = {}
    if prev is not None:
        for k, a in enumerate(prev):
            aliases[len(args)] = 1 + k
            args.append(a)
            in_specs.append(pl.BlockSpec(memory_space=pl.ANY))
    scratch = [pltpu.VMEM((ns, SUBLANES + CHUNK, CONV_CH), jnp.float32),
               pltpu.VMEM((ns, SSD_STATE, SSD_WIDTH), jnp.float32),
               pltpu.VMEM((ns, HG_HEADS, HG_HD, HG_HD), jnp.float32)]
    if t < CHUNK:
        scratch.append(pltpu.VMEM((ns, CHUNK, PROJ_COLS), jnp.float32))
    return pl.pallas_call(
        functools.partial(_mixer_kernel, t_rows=tblk, n_chunks=max(tblk // CHUNK, 1), n_seq=ns, layer=layer,
                          depth=depth, has_state=state_in is not None, has_prev=prev is not None),
        grid=(b // ns, t // tblk),
        in_specs=in_specs,
        out_specs=[pl.BlockSpec((ns, tblk, D_MIX), lambda i, j: (i, j, 0))] + out_state_specs,
        out_shape=[jax.ShapeDtypeStruct((b, t, D_MIX), jnp.bfloat16)] + state_shapes,
        scratch_shapes=scratch,
        input_output_aliases=aliases,
        compiler_params=pltpu.CompilerParams(
            dimension_semantics=("parallel", "arbitrary"), vmem_limit_bytes=VMEM_LIMIT),
        name="mixer",
    )(*args)


_W_IN_GROUPS = (
    (1024, C_Z), (1536, C_XBC), (16, C_SM + LANE_DT), (512, C_MQ), (512, C_MK), (512, C_MV),
    (4, C_SM + LANE_MI), (4, C_SM + LANE_MF), (512, C_MO), (512, C_HQ), (512, C_HK), (512, C_HI), (512, C_HG),
)
W_IN_COLS = sum(width for width, _ in _W_IN_GROUPS)
REGROUP_ROWS = 128


def _regroup_kernel(w_ref, o_ref):
    o_ref[:, C_SM:C_SM + 128] = jnp.zeros((REGROUP_ROWS, 128), jnp.bfloat16)
    src = 0
    for width, dst in _W_IN_GROUPS:
        o_ref[:, dst:dst + width] = w_ref[:, src:src + width]
        src += width


def _regroup_w_in(w):
    assert w.dtype == jnp.bfloat16
    depth = w.shape[0]
    return pl.pallas_call(
        _regroup_kernel,
        grid=(depth, D_MODEL // REGROUP_ROWS),
        in_specs=[pl.BlockSpec((None, REGROUP_ROWS, W_IN_COLS), lambda l, i: (l, i, 0))],
        out_specs=pl.BlockSpec((None, REGROUP_ROWS, W_COLS), lambda l, i: (l, i, 0)),
        out_shape=jax.ShapeDtypeStruct((depth, D_MODEL, W_COLS), jnp.bfloat16),
        compiler_params=pltpu.CompilerParams(
            dimension_semantics=("parallel", "parallel"), vmem_limit_bytes=VMEM_LIMIT),
        name="regroup_w_in",
    )(w)


def _small_params(dt_bias, a_log, ml_bi, ml_bf):
    sp = jnp.zeros((8, 128), jnp.float32)
    sp = sp.at[0, LANE_DT:LANE_DT + SSD_HEADS].set(dt_bias)
    sp = sp.at[1, LANE_DT:LANE_DT + SSD_HEADS].set(a_log)
    sp = sp.at[2, LANE_MF:LANE_MF + ML_HEADS].set(ml_bi)
    sp = sp.at[3, LANE_MF:LANE_MF + ML_HEADS].set(ml_bf)
    return sp


def _run_trunk(x, state_in, lp, norm_final, consts, tm_proj, tm_ffn):
    b, t, _ = x.shape
    assert t % CHUNK == 0 or t < CHUNK, "a partial chunk is only supported for single-chunk sequences"
    depth = len(lp)
    x2d = x.reshape(b * t, D_MODEL)
    states = None
    for l in range(depth):
        p = lp[l]
        params = (p["sp"], p["conv_w"], p["conv_b"], p["d_skip"], p["ssd_gain"], p["ml_gain"], p["hg_gain"])
        proj = _in_proj(x2d, p["norm_mix"], p["w_in"], p["hg_lb"], tm_proj, l)
        res = _mixer(proj.reshape(b, t, PROJ_COLS), state_in, states, params, consts, l, depth)
        states = res[1:]
        x2d = _out_ffn(x2d, res[0].reshape(b * t, D_MIX), p["w_out"], p["norm_ffn"], p["w_ffn_in"],
                       p["w_ffn_out"], norm_final, tm_ffn, l, l == depth - 1)
    conv, ssd, mc, mn, mm_p, hg = states
    return x2d.reshape(b, t, D_MODEL), (conv, ssd, mc, mn, mm_p[:, :, 0, LANE_MF:LANE_MF + ML_HEADS], hg)


def kernel(x_prompt, x_sample, state_conv, state_ssd, state_mlstm_c, state_mlstm_n, state_mlstm_m, state_hgrn,
           norm_mix, w_in, conv_w, conv_b, dt_bias, a_log, d_skip, ssd_gain, ml_bi, ml_bf, ml_gain,
           hg_lb, hg_gain, w_out, norm_ffn, w_ffn_in, w_ffn_out, norm_final):
    depth = w_in.shape[0]
    f32 = jnp.float32
    consts = _mixer_constants()
    bf16 = jnp.bfloat16
    w_in_b, w_out_b, w_fi_b, w_fo_b = (_regroup_w_in(w_in.astype(bf16)), w_out.astype(bf16), w_ffn_in.astype(bf16),
                                       w_ffn_out.astype(bf16))
    lp = []
    for l in range(depth):
        lp.append({
            "norm_mix": norm_mix[l].reshape(1, D_MODEL),
            "w_in": w_in_b,
            "sp": _small_params(dt_bias[l], a_log[l], ml_bi[l], ml_bf[l]),
            "conv_w": conv_w[l],
            "conv_b": conv_b[l].reshape(1, CONV_CH),
            "d_skip": jnp.repeat(d_skip[l], SSD_HEADDIM).reshape(1, SSD_WIDTH),
            "ssd_gain": ssd_gain[l].reshape(1, SSD_WIDTH),
            "ml_gain": ml_gain[l].reshape(1, ML_WIDTH),
            "hg_lb": hg_lb.astype(f32),
            "hg_gain": hg_gain[l].reshape(1, HG_WIDTH),
            "w_out": w_out_b,
            "norm_ffn": norm_ffn[l].reshape(1, D_MODEL),
            "w_ffn_in": w_fi_b,
            "w_ffn_out": w_fo_b,
        })
    nf = norm_final.reshape(1, D_MODEL)

    y_prompt, p_outs = _run_trunk(x_prompt, None, lp, nf, consts, 256, 512)
    bs = x_sample.shape[0]
    mm_p = jnp.pad(state_mlstm_m.astype(f32)[:, :, None, :],
                   ((0, 0), (0, 0), (0, 0), (LANE_MF, 128 - LANE_MF - ML_HEADS)))
    s_in = (state_conv, state_ssd, state_mlstm_c, state_mlstm_n, mm_p, state_hgrn)
    y_sample, s_outs = _run_trunk(x_sample, s_in, lp, nf, consts, 256, 512)
    return (y_prompt, y_sample) + p_outs + s_outs
```

```python
import functools

import numpy as np
import jax
import jax.numpy as jnp
from jax import lax
from jax.experimental import pallas as pl
from jax.experimental.pallas import tpu as pltpu

D_MODEL = 1024
CHUNK = 64
SSD_WIDTH = 1024
SSD_HEADDIM = 64
SSD_HEADS = 16
SSD_GROUPS = 2
SSD_STATE = 128
CONV_W = 4
CONV_CH = SSD_WIDTH + 2 * SSD_GROUPS * SSD_STATE
ML_WIDTH = 512
ML_HEADS = 4
ML_HD = 128
HG_WIDTH = 512
HG_HEADS = 4
HG_HD = 128
D_MIX = 2048
D_FF = 2816
EPS = 1e-6

C_Z = 0
C_XBC = 1024
C_MQ, C_MK, C_MV, C_MO = 2560, 3072, 3584, 4096
C_HQ, C_HK, C_HI, C_HG = 4608, 5120, 5632, 6144
C_SM = 6656
W_COLS = 6784
C_HL = 6784
PROJ_COLS = 7296
TIME_BLOCK = 256
SEQ_PER_STEP = 1
LOG2E = 1.4426950408889634
LANE_DT = 0
LANE_MI = 16
LANE_MF = 20

VMEM_LIMIT = 52 * 1024 * 1024
SUBLANES = 8
CONV_TAIL = SUBLANES - (CONV_W - 1)
NEG_INF = float("-inf")

HG_LEVELS = (32, 16, 8, 4, 2, 1)

_NT = (((1,), (1,)), ((), ()))


def _bf(x):
    return x.astype(jnp.bfloat16)


def _dot(a, b):
    return jnp.dot(a, b, preferred_element_type=jnp.float32)


def _dot_nt(a, b):
    return lax.dot_general(a, b, _NT, preferred_element_type=jnp.float32)


def _sigmoid(x):
    return 1.0 / (1.0 + jnp.exp(-x))


def _silu(x):
    return x * _sigmoid(x)


def _softplus(x):
    return jnp.maximum(x, 0.0) + jnp.log(1.0 + jnp.exp(-jnp.abs(x)))


def _split3(x):
    hi = _bf(x)
    r1 = x - hi.astype(jnp.float32)
    mid = _bf(r1)
    r2 = r1 - mid.astype(jnp.float32)
    return hi, mid, _bf(r2)


def _rms(x, gain):
    return x * lax.rsqrt(jnp.mean(x * x, axis=-1, keepdims=True) + EPS) * gain


def _hg_lower_bound(lb_all, layer):
    lb_e = jnp.exp(lb_all - jnp.max(lb_all, axis=0, keepdims=True))
    lb_soft = lb_e * (1.0 / jnp.sum(lb_e, axis=0, keepdims=True))
    return jnp.sum(lb_soft[0:layer + 1], axis=0, keepdims=True) - lb_soft[0:1]


def _causal_conv(cbuf_ref, cw_ref, cb_ref, rows):
    acc = cb_ref[...] + cbuf_ref[SUBLANES:SUBLANES + rows, :] * cw_ref[CONV_W - 1:CONV_W, :]
    for w in range(CONV_W - 1):
        acc = acc + cbuf_ref[CONV_TAIL + w:CONV_TAIL + w + rows, :] * cw_ref[w:w + 1, :]
    return acc


_PROJ_GROUPS = (
    (C_Z, 512, "silu"), (C_Z + 512, 512, "silu"),
    (C_XBC, 512, None), (C_XBC + 512, 512, None), (C_XBC + 1024, 512, None),
    (C_MQ, 512, None), (C_MK, 512, "kscale"), (C_MV, 512, None), (C_MO, 512, "sigmoid"),
    (C_HQ, 512, "silu"), (C_HK, 512, "fgate"), (C_HI, 512, None), (C_HG, 512, "silu"),
    (C_SM, 128, None),
)


def _proj_group(h, w_ref, lb_ref, layer, group, store):
    c0, width, act = group
    acc = _dot(h, w_ref[:, c0:c0 + width])
    if act == "silu":
        acc = _silu(acc)
    elif act == "sigmoid":
        acc = _sigmoid(acc)
    elif act == "kscale":
        acc = acc * (ML_HD ** -0.5)
    elif act == "fgate":
        lb = _hg_lower_bound(lb_ref[...], layer)
        fg = lb + (1.0 - lb) * _sigmoid(acc)
        store(C_HL, jnp.log2(fg))
        acc = 1.0 - fg
    store(c0, acc)


def _in_proj_kernel(x_ref, g_ref, w_ref, lb_ref, o_ref, *, layer):
    h = _bf(_rms(x_ref[...], g_ref[...]))

    def store(c0, val):
        o_ref[:, c0:c0 + val.shape[1]] = val

    for group in _PROJ_GROUPS:
        _proj_group(h, w_ref, lb_ref, layer, group, store)


def _in_proj(x2d, gain, w, hg_lb, tm, layer):
    n = x2d.shape[0]
    c2 = lambda i: (0, 0)
    return pl.pallas_call(
        functools.partial(_in_proj_kernel, layer=layer),
        grid=(n // tm,),
        in_specs=[
            pl.BlockSpec((tm, D_MODEL), lambda i: (i, 0)),
            pl.BlockSpec((1, D_MODEL), c2),
            pl.BlockSpec((None, D_MODEL, W_COLS), lambda i: (layer, 0, 0), pipeline_mode=pl.Buffered(1)),
            pl.BlockSpec(hg_lb.shape, c2),
        ],
        out_specs=pl.BlockSpec((tm, PROJ_COLS), lambda i: (i, 0)),
        out_shape=jax.ShapeDtypeStruct((n, PROJ_COLS), jnp.float32),
        compiler_params=pltpu.CompilerParams(
            dimension_semantics=("parallel",), vmem_limit_bytes=VMEM_LIMIT),
        name="in_proj",
    )(x2d, gain, w, hg_lb)


FF_TILE = 256


def _out_ffn_kernel(x_ref, mix_ref, wo_ref, gn_ref, wi_ref, wf_ref, gf_ref, o_ref, *, final_norm):
    x1 = x_ref[...] + _dot(mix_ref[...], wo_ref[...])
    h2 = _bf(_rms(x1, gn_ref[...]))
    acc = x1
    for j in range(D_FF // FF_TILE):
        g = _dot(h2, wi_ref[:, j * FF_TILE:(j + 1) * FF_TILE])
        u = _dot(h2, wi_ref[:, D_FF + j * FF_TILE:D_FF + (j + 1) * FF_TILE])
        acc = acc + _dot(_bf(_silu(g) * u), wf_ref[j * FF_TILE:(j + 1) * FF_TILE, :])
    if final_norm:
        acc = _rms(acc, gf_ref[...])
    o_ref[...] = acc


def _out_ffn(x2d, mix2d, w_out, g_ffn, w_fi, w_fo, g_final, tm, layer, final_norm):
    n = x2d.shape[0]
    const = lambda i: (0, 0)
    lyr = lambda i: (layer, 0, 0)
    one = pl.Buffered(1)
    return pl.pallas_call(
        functools.partial(_out_ffn_kernel, final_norm=final_norm),
        grid=(n // tm,),
        in_specs=[
            pl.BlockSpec((tm, D_MODEL), lambda i: (i, 0)),
            pl.BlockSpec((tm, D_MIX), lambda i: (i, 0)),
            pl.BlockSpec((None, D_MIX, D_MODEL), lyr, pipeline_mode=one),
            pl.BlockSpec((1, D_MODEL), const),
            pl.BlockSpec((None, D_MODEL, 2 * D_FF), lyr, pipeline_mode=one),
            pl.BlockSpec((None, D_FF, D_MODEL), lyr, pipeline_mode=one),
            pl.BlockSpec((1, D_MODEL), const),
        ],
        out_specs=pl.BlockSpec((tm, D_MODEL), lambda i: (i, 0)),
        out_shape=jax.ShapeDtypeStruct((n, D_MODEL), jnp.float32),
        compiler_params=pltpu.CompilerParams(
            dimension_semantics=("parallel",), vmem_limit_bytes=VMEM_LIMIT),
        name="out_ffn",
    )(x2d, mix2d, w_out, g_ffn, w_fi, w_fo, g_final)


def _mixer_constants():
    L = CHUNK
    e3 = np.zeros((128, SSD_WIDTH), np.float32)
    for piece in range(3):
        for h in range(SSD_HEADS):
            e3[piece * SSD_HEADS + h, h * SSD_HEADDIM:(h + 1) * SSD_HEADDIM] = 1.0

    l = np.arange(L)[:, None]
    c = np.arange(SSD_WIDTH)[None, :]
    dsel = ((c % L) == l).astype(np.float32)
    s2 = np.arange(128)[None, :] % L
    causal2 = (s2 <= l).astype(np.float32)

    t = np.arange(L)
    blocks = []
    masks = []
    isq = np.zeros((L, 128), np.float32)
    for li, m in enumerate(HG_LEVELS):
        cm = np.zeros((L, L), np.float32)
        start = (t // (2 * m)) * (2 * m)
        mid = start + m - 1
        query = (t - start) >= m
        for s in range(L):
            if query[s]:
                cm[s, mid[s] + 1:s + 1] = 1.0
            else:
                cm[s, s + 1:mid[s] + 1] = 1.0
        blocks.append(cm)
        same = (start[:, None] == start[None, :])
        mk = same & query[:, None] & (~query[None, :])
        masks.append(np.concatenate([mk, mk], axis=1).astype(np.float32))
        isq[:, li] = query.astype(np.float32)
    tri = (t[None, :] <= t[:, None]).astype(np.float32)
    blocks.append(tri)
    blocks.append((t[None, :] > t[:, None]).astype(np.float32))
    ce = np.concatenate(blocks, axis=0)
    ce3 = np.concatenate([ce, ce, ce], axis=1)
    diag = (t[:, None] == t[None, :])
    masks.append(np.concatenate([diag, diag], axis=1).astype(np.float32))
    hmask = np.stack(masks, axis=0)
    return (jnp.asarray(e3, jnp.bfloat16), jnp.asarray(dsel), jnp.asarray(causal2),
            jnp.asarray(ce3, jnp.bfloat16), jnp.asarray(hmask), jnp.asarray(isq))


def _scan_time(x, op, identity, row):
    sh = 1
    while sh < x.shape[0]:
        r = pltpu.roll(x, sh, axis=0)
        x = op(x, jnp.where(row >= sh, r, identity))
        sh *= 2
    return x


def _blockdiag_rows(a, b):
    z = jnp.zeros_like(a)
    return jnp.concatenate([jnp.concatenate([a, z], axis=1), jnp.concatenate([z, b], axis=1)], axis=0)


def _mixer_chunk(col, put, sp_ref, cw_ref, cb_ref, dsk_ref, sg_ref, mlg_ref, hgg_ref,
                 e3_ref, dsel_ref, causal_ref, ce3_ref, hmask_ref, isq_ref,
                 mc_ref, mn_ref, mm_ref, ht_ref, hgt_ref, cbuf_ref, *, t_valid):
    L = CHUNK
    row = lax.broadcasted_iota(jnp.int32, (L, 128), 0)
    lane = lax.broadcasted_iota(jnp.int32, (L, 128), 1)
    lo_half = lane < 64
    causal2 = causal_ref[...] > 0.5

    nl = len(HG_LEVELS)
    ml_pairs = range(ML_HEADS // 2)
    hg_pairs = range(HG_HEADS // 2)

    sm = col(C_SM, 128)
    dt = _softplus(sm + sp_ref[0:1, :])
    a_neg2 = jnp.where(lane[0:1] < SSD_HEADS, -LOG2E * jnp.exp(sp_ref[1:2, :]), 0.0)
    ig = pltpu.roll(sm, LANE_MF - LANE_MI, axis=1) + sp_ref[2:3, :]
    lf = -_softplus(-(sm + sp_ref[3:4, :]))
    if t_valid < L:
        ok = row < t_valid
        dt = jnp.where(ok, dt, 0.0)
        ig = jnp.where(ok, ig, NEG_INF)
        lf = jnp.where(ok, lf, 0.0)
    ml_lane = (lane >= LANE_MF) & (lane < LANE_MF + ML_HEADS)
    cs = _scan_time(jnp.where(lane < SSD_HEADS, dt * a_neg2, jnp.where(ml_lane, lf, 0.0)), jnp.add, 0.0, row)
    acum2 = cs
    u = jnp.where(ml_lane, ig - cs, 0.0)
    m_prev = mm_ref[...]
    m_run = jnp.maximum(_scan_time(u, jnp.maximum, NEG_INF, row), m_prev)
    m_i = cs + m_run
    m_last = m_run[L - 1:L, :]
    w_inter = jnp.exp(m_prev - m_run)
    inv_floor = jnp.exp(-m_i)
    wk_s = jnp.exp(u - m_last)
    sc_s = jnp.exp(m_prev - m_last)

    hi, mid, lo = _split3(jnp.concatenate([dt, acum2], axis=0))
    lane2 = lax.broadcasted_iota(jnp.int32, (2 * L, 128), 1)
    packed = jnp.where(
        lane2 < SSD_HEADS, hi.astype(jnp.float32),
        jnp.where(lane2 < 2 * SSD_HEADS, pltpu.roll(mid.astype(jnp.float32), SSD_HEADS, axis=1),
                  jnp.where(lane2 < 3 * SSD_HEADS, pltpu.roll(lo.astype(jnp.float32), 2 * SSD_HEADS, axis=1), 0.0)))
    ex = _dot(_bf(packed), e3_ref[...])
    glog2 = col(C_HL, HG_WIDTH)
    if t_valid < L:
        glog2 = jnp.where(lax.broadcasted_iota(jnp.int32, (L, HG_WIDTH), 0) < t_valid, glog2, 0.0)
    ghi, gmid, glo = _split3(glog2)
    exps = _dot(ce3_ref[...], jnp.concatenate([ghi, gmid, glo], axis=0))
    yield

    xbc = col(C_XBC, CONV_CH)
    cbuf_ref[SUBLANES:SUBLANES + L, :] = xbc
    xbc_a = _silu(_causal_conv(cbuf_ref, cw_ref, cb_ref, L))
    cbuf_ref[CONV_TAIL:SUBLANES, :] = xbc[t_valid - (CONV_W - 1):t_valid, :]
    xs = xbc_a[:, 0:SSD_WIDTH]
    yield

    ml_q2 = [_bf(col(C_MQ + pp * 256, 256)) for pp in ml_pairs]
    ml_k = [col(C_MK + h * 128, 128) for h in range(ML_HEADS)]
    ml_v = [col(C_MV + h * 128, 128) for h in range(ML_HEADS)]
    ml_s2raw = [_dot_nt(ml_q2[pp], _bf(_blockdiag_rows(ml_k[2 * pp], ml_k[2 * pp + 1]))) for pp in ml_pairs]
    ml_qc = [_dot(ml_q2[h // 2][:, (h % 2) * 128:(h % 2 + 1) * 128], _bf(mc_ref[h])) for h in range(ML_HEADS)]

    kk = col(C_HK, HG_WIDTH)
    qq = col(C_HQ, HG_WIDTH)
    vv = col(C_HI, HG_WIDTH)
    qq_b, kk_b, vv_b = _bf(qq), _bf(kk), _bf(vv)
    hg_sl = [slice(h * 128, (h + 1) * 128) for h in range(HG_HEADS)]
    hg_adiag = [_dot_nt(qq_b[:, pp * 256:(pp + 1) * 256],
                        _blockdiag_rows(kk_b[:, hg_sl[2 * pp]], kk_b[:, hg_sl[2 * pp + 1]])) for pp in hg_pairs]

    ssd_bm, ssd_cb2, ssd_yi = [], [], []
    for g in range(SSD_GROUPS):
        bm = xbc_a[:, SSD_WIDTH + g * SSD_STATE:SSD_WIDTH + (g + 1) * SSD_STATE]
        cm = xbc_a[:, SSD_WIDTH + (SSD_GROUPS + g) * SSD_STATE:SSD_WIDTH + (SSD_GROUPS + g + 1) * SSD_STATE]
        bm_b, cm_b = _bf(bm), _bf(cm)
        ssd_bm.append(bm)
        ssd_cb2.append(_dot_nt(cm_b, jnp.concatenate([bm_b, bm_b], axis=0)))
        ssd_yi.append(_dot(cm_b, _bf(ht_ref[:, g * 512:(g + 1) * 512])))
    yield

    dt_f, ac_f = ex[0:L], ex[L:2 * L]
    eac_f = jnp.exp2(ac_f)
    dend_f = jnp.exp2(ac_f[L - 1:L, :] - ac_f)
    xt = xs * dt_f
    ctr = jnp.sum(jnp.where(dsel_ref[...] > 0.5, ac_f, 0.0), axis=0, keepdims=True)
    xdec = _bf(xt * dend_f)
    eac_end = eac_f[L - 1:L, :]
    yield

    qk_lv = []
    for li in range(nl):
        isq = isq_ref[:, li:li + 1] > 0.5
        qk_lv.append(_bf(jnp.where(isq, qq, kk) * jnp.exp2(exps[li * L:(li + 1) * L])))
    gc = exps[nl * L:(nl + 1) * L]
    gend = exps[(nl + 1) * L:(nl + 2) * L]
    q_dec = _bf(qq * jnp.exp2(gc))
    k_dec = _bf(kk * jnp.exp2(gend))
    s_dec = jnp.exp2(gc[L - 1:L, :])
    yield

    hg_masks = [hmask_ref[li] > 0.5 for li in range(nl + 1)]
    hg_a2 = []
    for pp in hg_pairs:
        a2 = jnp.where(hg_masks[nl], hg_adiag[pp], 0.0)
        for li in range(nl):
            x = qk_lv[li]
            a_l = _dot_nt(x[:, pp * 256:(pp + 1) * 256], _blockdiag_rows(x[:, hg_sl[2 * pp]], x[:, hg_sl[2 * pp + 1]]))
            a2 = jnp.where(hg_masks[li], a_l, a2)
        hg_a2.append(_bf(a2))
    hg_oi = [_dot_nt(q_dec[:, hg_sl[h]], _bf(hgt_ref[h])) for h in range(HG_HEADS)]
    yield

    dsel128 = dsel_ref[:, 0:128] > 0.5
    ml_s2, ml_dens = [], []
    for pp in ml_pairs:
        l0, l1 = LANE_MF + 2 * pp, LANE_MF + 2 * pp + 1
        m_row = jnp.where(lo_half, m_run[:, l0:l0 + 1], m_run[:, l1:l1 + 1])
        u_row = jnp.where(lo_half, u[:, l0:l0 + 1], u[:, l1:l1 + 1])
        u_col = jnp.sum(jnp.where(dsel128, u_row, 0.0), axis=0, keepdims=True)
        s2 = ml_s2raw[pp] * jnp.exp(jnp.where(causal2, u_col - m_row, NEG_INF))
        ml_s2.append(s2)
        ml_dens.append((jnp.sum(jnp.where(lo_half, s2, 0.0), axis=-1, keepdims=True),
                        jnp.sum(jnp.where(lo_half, 0.0, s2), axis=-1, keepdims=True)))
    ml_num2 = [_dot(_bf(ml_s2[pp]), _bf(_blockdiag_rows(ml_v[2 * pp], ml_v[2 * pp + 1]))) for pp in ml_pairs]
    yield

    y_parts = []
    for g in range(SSD_GROUPS):
        gs = slice(g * 512, (g + 1) * 512)
        for pp in range(4):
            ps = slice((g * 4 + pp) * 128, (g * 4 + pp + 1) * 128)
            seg = ac_f[:, ps] - ctr[:, ps]
            m2 = _bf(ssd_cb2[g] * jnp.exp2(jnp.where(causal2, seg, NEG_INF)))
            xp = xt[:, ps]
            xb = _bf(jnp.concatenate([jnp.where(lo_half, xp, 0.0), jnp.where(lo_half, 0.0, xp)], axis=0))
            y_parts.append(_dot(m2, xb) + eac_f[:, ps] * ssd_yi[g][:, pp * 128:(pp + 1) * 128])
        ht_ref[:, gs] = ht_ref[:, gs] * eac_end[:, gs] + _dot(_bf(ssd_bm[g].T), xdec[:, gs])
        yield

    hg_o2 = [_dot(hg_a2[pp], _blockdiag_rows(vv_b[:, hg_sl[2 * pp]], vv_b[:, hg_sl[2 * pp + 1]]))
             for pp in hg_pairs]

    for h in range(ML_HEADS):
        pp, j, ln = h // 2, h % 2, LANE_MF + h
        qh = col(C_MQ + h * 128, 128)
        wi = w_inter[:, ln:ln + 1]
        num = ml_num2[pp][:, j * 128:(j + 1) * 128] + wi * ml_qc[h]
        qn = jnp.sum(qh * mn_ref[h:h + 1, :], axis=-1, keepdims=True)
        den = ml_dens[pp][j] + wi * qn
        hval = num * (1.0 / jnp.maximum(jnp.abs(den), inv_floor[:, ln:ln + 1]))
        hn = _rms(hval, mlg_ref[:, h * 128:(h + 1) * 128])
        put(SSD_WIDTH + h * 128, _bf(col(C_MO + h * 128, 128) * hn))
        kw = ml_k[h] * wk_s[:, ln:ln + 1]
        sc = sc_s[:, ln:ln + 1]
        mc_ref[h] = sc * mc_ref[h] + _dot(_bf(kw.T), _bf(ml_v[h]))
        mn_ref[h:h + 1, :] = sc * mn_ref[h:h + 1, :] + jnp.sum(kw, axis=0, keepdims=True)
        if h % 2 == 1:
            yield
    mm_ref[...] = jnp.where(ml_lane[0:1], m_i[L - 1:L, :], 0.0)

    y = jnp.concatenate(y_parts, axis=1) + dsk_ref[...] * xs
    y_ssd = _rms(y * col(C_Z, SSD_WIDTH), sg_ref[...])
    put(0, _bf(y_ssd))
    yield

    for h in range(HG_HEADS):
        hs = hg_sl[h]
        o = hg_o2[h // 2][:, (h % 2) * 128:(h % 2 + 1) * 128] + hg_oi[h]
        on = _rms(o, hgg_ref[:, hs])
        put(SSD_WIDTH + ML_WIDTH + h * 128, _bf(on * col(C_HG + h * 128, 128)))
        hgt_ref[h] = hgt_ref[h] * s_dec[:, hs] + _dot(_bf(vv[:, hs].T), k_dec[:, hs])


def _lockstep(gens):
    gens = list(gens)
    while gens:
        alive = []
        for g in gens:
            try:
                next(g)
                alive.append(g)
            except StopIteration:
                pass
        gens = alive


def _mixer_kernel(*refs, t_rows, n_chunks, n_seq, layer, depth, has_state, has_prev):
    it = iter(refs)
    proj_ref = next(it)
    state0 = [next(it) for _ in range(6)] if has_state else None
    sp_ref, cw_ref, cb_ref, dsk_ref, sg_ref, mlg_ref, hgg_ref = (next(it) for _ in range(7))
    e3_ref, dsel_ref, causal_ref, ce3_ref, hmask_ref, isq_ref = (next(it) for _ in range(6))
    if has_prev:
        for _ in range(6):
            next(it)
    mix_ref, conv_ref, ssd_ref, mc_ref, mn_ref, mm_ref, hg_ref = (next(it) for _ in range(7))
    cbuf_ref, ht_ref, hgt_ref = (next(it) for _ in range(3))
    padded = t_rows < CHUNK
    pbuf_ref = next(it) if padded else None
    tb = pl.program_id(1)
    li = 0 if has_prev else layer
    out_refs = (conv_ref, ssd_ref, mc_ref, mn_ref, mm_ref, hg_ref)

    @pl.when(tb == 0)
    def _init():
        if not has_prev:
            for other in range(depth):
                if other != layer:
                    for r in out_refs:
                        r[other] = jnp.zeros(r.shape[1:], jnp.float32)
        for s in range(n_seq):
            if has_state:
                conv0_ref, ssd0_ref, mc0_ref, mn0_ref, mm0_ref, hg0_ref = state0
                cbuf_ref[s, CONV_TAIL:SUBLANES, :] = conv0_ref[0, s]
                ht_ref[s] = ssd0_ref[0, s].reshape(SSD_WIDTH, SSD_STATE).T
                mc_ref[li, s] = mc0_ref[0, s]
                mn_ref[li, s] = mn0_ref[0, s]
                mm_ref[li, s] = mm0_ref[0, s]
                for h in range(HG_HEADS):
                    hgt_ref[s, h] = hg0_ref[0, s, h].T
            else:
                cbuf_ref[s, CONV_TAIL:SUBLANES, :] = jnp.zeros((CONV_W - 1, CONV_CH), jnp.float32)
                ht_ref[s] = jnp.zeros(ht_ref.shape[1:], jnp.float32)
                mc_ref[li, s] = jnp.zeros(mc_ref.shape[2:], jnp.float32)
                mn_ref[li, s] = jnp.zeros(mn_ref.shape[2:], jnp.float32)
                mm_ref[li, s] = jnp.zeros(mm_ref.shape[2:], jnp.float32)
                hgt_ref[s] = jnp.zeros(hgt_ref.shape[1:], jnp.float32)

    def chunk(s, col, put):
        return _mixer_chunk(
            col, put, sp_ref=sp_ref, cw_ref=cw_ref, cb_ref=cb_ref, dsk_ref=dsk_ref, sg_ref=sg_ref, mlg_ref=mlg_ref,
            hgg_ref=hgg_ref, e3_ref=e3_ref, dsel_ref=dsel_ref, causal_ref=causal_ref, ce3_ref=ce3_ref,
            hmask_ref=hmask_ref, isq_ref=isq_ref, mc_ref=mc_ref.at[li, s], mn_ref=mn_ref.at[li, s],
            mm_ref=mm_ref.at[li, s], ht_ref=ht_ref.at[s], hgt_ref=hgt_ref.at[s], cbuf_ref=cbuf_ref.at[s],
            t_valid=t_rows if padded else CHUNK)

    if padded:
        gens = []
        for s in range(n_seq):
            pbuf_ref[s, 0:t_rows, :] = proj_ref[s]
            pbuf_ref[s, t_rows:CHUNK, :] = jnp.zeros((CHUNK - t_rows, PROJ_COLS), jnp.float32)

            def col(c0, width, s=s):
                return pbuf_ref[s, :, c0:c0 + width]

            def put(c0, val, s=s):
                mix_ref[s, :, c0:c0 + val.shape[1]] = val[0:t_rows]

            gens.append(chunk(s, col, put))
        _lockstep(gens)
    else:
        def body(ci, carry):
            rows = pl.ds(pl.multiple_of(ci * CHUNK, CHUNK), CHUNK)
            gens = []
            for s in range(n_seq):
                def col(c0, width, s=s):
                    return proj_ref[s, rows, c0:c0 + width]

                def put(c0, val, s=s):
                    mix_ref[s, rows, c0:c0 + val.shape[1]] = val

                gens.append(chunk(s, col, put))
            _lockstep(gens)
            return carry
        lax.fori_loop(0, n_chunks, body, 0, unroll=2 if n_chunks % 2 == 0 else 1)

    @pl.when(tb == pl.num_programs(1) - 1)
    def _fin():
        for s in range(n_seq):
            conv_ref[li, s] = cbuf_ref[s, CONV_TAIL:SUBLANES, :]
            ssd_ref[li, s] = ht_ref[s].T.reshape(SSD_HEADS, SSD_HEADDIM, SSD_STATE)
            for h in range(HG_HEADS):
                hg_ref[li, s, h] = hgt_ref[s, h].T


def _mixer(proj, state_in, prev, params, consts, layer, depth):
    b, t, _ = proj.shape
    if t < CHUNK:
        tblk = t
    else:
        tblk = min(TIME_BLOCK, t)
        assert t % tblk == 0 and tblk % CHUNK == 0
    ns = SEQ_PER_STEP
    assert b % ns == 0
    c2 = lambda i, j: (0, 0)
    c3 = lambda i, j: (0, 0, 0)
    state_dims = [(CONV_W - 1, CONV_CH), (SSD_HEADS, SSD_HEADDIM, SSD_STATE), (ML_HEADS, ML_HD, ML_HD),
                  (ML_HEADS, ML_HD), (1, 128), (HG_HEADS, HG_HD, HG_HD)]

    def state_spec(dims, layers, first):
        return pl.BlockSpec((layers, ns) + dims, lambda i, j: (first, i) + (0,) * len(dims))

    in_state_specs = [state_spec(d, 1, layer) for d in state_dims]
    out_state_specs = [state_spec(d, 1, layer) if prev is not None else state_spec(d, depth, 0) for d in state_dims]
    state_shapes = [
        jax.ShapeDtypeStruct((depth, b, CONV_W - 1, CONV_CH), jnp.float32),
        jax.ShapeDtypeStruct((depth, b, SSD_HEADS, SSD_HEADDIM, SSD_STATE), jnp.float32),
        jax.ShapeDtypeStruct((depth, b, ML_HEADS, ML_HD, ML_HD), jnp.float32),
        jax.ShapeDtypeStruct((depth, b, ML_HEADS, ML_HD), jnp.float32),
        jax.ShapeDtypeStruct((depth, b, 1, 128), jnp.float32),
        jax.ShapeDtypeStruct((depth, b, HG_HEADS, HG_HD, HG_HD), jnp.float32),
    ]
    args = [proj]
    in_specs = [pl.BlockSpec((ns, tblk, PROJ_COLS), lambda i, j: (i, j, 0))]
    if state_in is not None:
        args += list(state_in)
        in_specs += in_state_specs
    args += list(params) + list(consts)
    in_specs += [pl.BlockSpec(a.shape, c3 if a.ndim == 3 else c2) for a in list(params) + list(consts)]
    aliases = {}
    if prev is not None:
        for k, a in enumerate(prev):
            aliases[len(args)] = 1 + k
            args.append(a)
            in_specs.append(pl.BlockSpec(memory_space=pl.ANY))
    scratch = [pltpu.VMEM((ns, SUBLANES + CHUNK, CONV_CH), jnp.float32),
               pltpu.VMEM((ns, SSD_STATE, SSD_WIDTH), jnp.float32),
               pltpu.VMEM((ns, HG_HEADS, HG_HD, HG_HD), jnp.float32)]
    if t < CHUNK:
        scratch.append(pltpu.VMEM((ns, CHUNK, PROJ_COLS), jnp.float32))
    return pl.pallas_call(
        functools.partial(_mixer_kernel, t_rows=tblk, n_chunks=max(tblk // CHUNK, 1), n_seq=ns, layer=layer,
                          depth=depth, has_state=state_in is not None, has_prev=prev is not None),
        grid=(b // ns, t // tblk),
        in_specs=in_specs,
        out_specs=[pl.BlockSpec((ns, tblk, D_MIX), lambda i, j: (i, j, 0))] + out_state_specs,
        out_shape=[jax.ShapeDtypeStruct((b, t, D_MIX), jnp.bfloat16)] + state_shapes,
        scratch_shapes=scratch,
        input_output_aliases=aliases,
        compiler_params=pltpu.CompilerParams(
            dimension_semantics=("parallel", "arbitrary"), vmem_limit_bytes=VMEM_LIMIT),
        name="mixer",
    )(*args)


_W_IN_GROUPS = (
    (1024, C_Z), (1536, C_XBC), (16, C_SM + LANE_DT), (512, C_MQ), (512, C_MK), (512, C_MV),
    (4, C_SM + LANE_MI), (4, C_SM + LANE_MF), (512, C_MO), (512, C_HQ), (512, C_HK), (512, C_HI), (512, C_HG),
)
W_IN_COLS = sum(width for width, _ in _W_IN_GROUPS)
REGROUP_ROWS = 128


def _regroup_kernel(w_ref, o_ref):
    o_ref[:, C_SM:C_SM + 128] = jnp.zeros((REGROUP_ROWS, 128), jnp.bfloat16)
    src = 0
    for width, dst in _W_IN_GROUPS:
        o_ref[:, dst:dst + width] = w_ref[:, src:src + width].astype(jnp.bfloat16)
        src += width


def _regroup_w_in(w):
    depth = w.shape[0]
    return pl.pallas_call(
        _regroup_kernel,
        grid=(depth, D_MODEL // REGROUP_ROWS),
        in_specs=[pl.BlockSpec((None, REGROUP_ROWS, W_IN_COLS), lambda l, i: (l, i, 0))],
        out_specs=pl.BlockSpec((None, REGROUP_ROWS, W_COLS), lambda l, i: (l, i, 0)),
        out_shape=jax.ShapeDtypeStruct((depth, D_MODEL, W_COLS), jnp.bfloat16),
        compiler_params=pltpu.CompilerParams(
            dimension_semantics=("parallel", "parallel"), vmem_limit_bytes=VMEM_LIMIT),
        name="regroup_w_in",
    )(w)


def _small_params(dt_bias, a_log, ml_bi, ml_bf):
    sp = jnp.zeros((8, 128), jnp.float32)
    sp = sp.at[0, LANE_DT:LANE_DT + SSD_HEADS].set(dt_bias)
    sp = sp.at[1, LANE_DT:LANE_DT + SSD_HEADS].set(a_log)
    sp = sp.at[2, LANE_MF:LANE_MF + ML_HEADS].set(ml_bi)
    sp = sp.at[3, LANE_MF:LANE_MF + ML_HEADS].set(ml_bf)
    return sp


def _run_trunk(x, state_in, lp, norm_final, consts, tm_proj, tm_ffn):
    b, t, _ = x.shape
    assert t % CHUNK == 0 or t < CHUNK, "a partial chunk is only supported for single-chunk sequences"
    depth = len(lp)
    x2d = x.reshape(b * t, D_MODEL)
    states = None
    for l in range(depth):
        p = lp[l]
        params = (p["sp"], p["conv_w"], p["conv_b"], p["d_skip"], p["ssd_gain"], p["ml_gain"], p["hg_gain"])
        proj = _in_proj(x2d, p["norm_mix"], p["w_in"], p["hg_lb"], tm_proj, l)
        res = _mixer(proj.reshape(b, t, PROJ_COLS), state_in, states, params, consts, l, depth)
        states = res[1:]
        x2d = _out_ffn(x2d, res[0].reshape(b * t, D_MIX), p["w_out"], p["norm_ffn"], p["w_ffn_in"],
                       p["w_ffn_out"], norm_final, tm_ffn, l, l == depth - 1)
    conv, ssd, mc, mn, mm_p, hg = states
    return x2d.reshape(b, t, D_MODEL), (conv, ssd, mc, mn, mm_p[:, :, 0, LANE_MF:LANE_MF + ML_HEADS], hg)


def kernel(x_prompt, x_sample, state_conv, state_ssd, state_mlstm_c, state_mlstm_n, state_mlstm_m, state_hgrn,
           norm_mix, w_in, conv_w, conv_b, dt_bias, a_log, d_skip, ssd_gain, ml_bi, ml_bf, ml_gain,
           hg_lb, hg_gain, w_out, norm_ffn, w_ffn_in, w_ffn_out, norm_final):
    depth = w_in.shape[0]
    f32 = jnp.float32
    consts = _mixer_constants()
    bf16 = jnp.bfloat16
    w_in_b, w_out_b, w_fi_b, w_fo_b = (_regroup_w_in(w_in), w_out.astype(bf16), w_ffn_in.astype(bf16),
                                       w_ffn_out.astype(bf16))
    lp = []
    for l in range(depth):
        lp.append({
            "norm_mix": norm_mix[l].reshape(1, D_MODEL),
            "w_in": w_in_b,
            "sp": _small_params(dt_bias[l], a_log[l], ml_bi[l], ml_bf[l]),
            "conv_w": conv_w[l],
            "conv_b": conv_b[l].reshape(1, CONV_CH),
            "d_skip": jnp.repeat(d_skip[l], SSD_HEADDIM).reshape(1, SSD_WIDTH),
            "ssd_gain": ssd_gain[l].reshape(1, SSD_WIDTH),
            "ml_gain": ml_gain[l].reshape(1, ML_WIDTH),
            "hg_lb": hg_lb.astype(f32),
            "hg_gain": hg_gain[l].reshape(1, HG_WIDTH),
            "w_out": w_out_b,
            "norm_ffn": norm_ffn[l].reshape(1, D_MODEL),
            "w_ffn_in": w_fi_b,
            "w_ffn_out": w_fo_b,
        })
    nf = norm_final.reshape(1, D_MODEL)

    y_prompt, p_outs = _run_trunk(x_prompt, None, lp, nf, consts, 256, 512)
    bs = x_sample.shape[0]
    mm_p = jnp.pad(state_mlstm_m.astype(f32)[:, :, None, :],
                   ((0, 0), (0, 0), (0, 0), (LANE_MF, 128 - LANE_MF - ML_HEADS)))
    s_in = (state_conv, state_ssd, state_mlstm_c, state_mlstm_n, mm_p, state_hgrn)
    y_sample, s_outs = _run_trunk(x_sample, s_in, lp, nf, consts, 256, 512)
    return (y_prompt, y_sample) + p_outs + s_outs
```

```python
import functools

import numpy as np
import jax
import jax.numpy as jnp
from jax import lax
from jax.experimental import pallas as pl
from jax.experimental.pallas import tpu as pltpu

D_MODEL = 1024
CHUNK = 64
SSD_WIDTH = 1024
SSD_HEADDIM = 64
SSD_HEADS = 16
SSD_GROUPS = 2
SSD_STATE = 128
CONV_W = 4
CONV_CH = SSD_WIDTH + 2 * SSD_GROUPS * SSD_STATE
ML_WIDTH = 512
ML_HEADS = 4
ML_HD = 128
HG_WIDTH = 512
HG_HEADS = 4
HG_HD = 128
D_MIX = 2048
D_FF = 2816
EPS = 1e-6

C_Z = 0
C_XBC = 1024
C_MQ, C_MK, C_MV, C_MO = 2560, 3072, 3584, 4096
C_HQ, C_HK, C_HI, C_HG = 4608, 5120, 5632, 6144
C_SM = 6656
W_COLS = 6784
C_HL = 6784
PROJ_COLS = 7296
TIME_BLOCK = 256
SEQ_PER_STEP = 1
RING_SLOTS = 3
LOG2E = 1.4426950408889634
LANE_DT = 0
LANE_MI = 16
LANE_MF = 20

VMEM_LIMIT = 52 * 1024 * 1024
SUBLANES = 8
CONV_TAIL = SUBLANES - (CONV_W - 1)
NEG_INF = float("-inf")

HG_LEVELS = (32, 16, 8, 4, 2, 1)

_NT = (((1,), (1,)), ((), ()))


def _bf(x):
    return x.astype(jnp.bfloat16)


def _dot(a, b):
    return jnp.dot(a, b, preferred_element_type=jnp.float32)


def _dot_nt(a, b):
    return lax.dot_general(a, b, _NT, preferred_element_type=jnp.float32)


def _sigmoid(x):
    return 1.0 / (1.0 + jnp.exp(-x))


def _silu(x):
    return x * _sigmoid(x)


def _softplus(x):
    return jnp.maximum(x, 0.0) + jnp.log(1.0 + jnp.exp(-jnp.abs(x)))


def _split3(x):
    hi = _bf(x)
    r1 = x - hi.astype(jnp.float32)
    mid = _bf(r1)
    r2 = r1 - mid.astype(jnp.float32)
    return hi, mid, _bf(r2)


def _rms(x, gain):
    return x * lax.rsqrt(jnp.mean(x * x, axis=-1, keepdims=True) + EPS) * gain


def _hg_lower_bound(lb_all, layer):
    lb_e = jnp.exp(lb_all - jnp.max(lb_all, axis=0, keepdims=True))
    lb_soft = lb_e * (1.0 / jnp.sum(lb_e, axis=0, keepdims=True))
    return jnp.sum(lb_soft[0:layer + 1], axis=0, keepdims=True) - lb_soft[0:1]


def _causal_conv(cbuf_ref, cw_ref, cb_ref, rows):
    acc = cb_ref[...] + cbuf_ref[SUBLANES:SUBLANES + rows, :] * cw_ref[CONV_W - 1:CONV_W, :]
    for w in range(CONV_W - 1):
        acc = acc + cbuf_ref[CONV_TAIL + w:CONV_TAIL + w + rows, :] * cw_ref[w:w + 1, :]
    return acc


_PROJ_GROUPS = (
    (C_Z, 512, "silu"), (C_Z + 512, 512, "silu"),
    (C_XBC, 512, None), (C_XBC + 512, 512, None), (C_XBC + 1024, 512, None),
    (C_MQ, 512, None), (C_MK, 512, "kscale"), (C_MV, 512, None), (C_MO, 512, "sigmoid"),
    (C_HQ, 512, "silu"), (C_HK, 512, "fgate"), (C_HI, 512, None), (C_HG, 512, "silu"),
    (C_SM, 128, None),
)


def _proj_group(h, w_ref, lb_ref, layer, group, store):
    c0, width, act = group
    acc = _dot(h, w_ref[:, c0:c0 + width])
    if act == "silu":
        acc = _silu(acc)
    elif act == "sigmoid":
        acc = _sigmoid(acc)
    elif act == "kscale":
        acc = acc * (ML_HD ** -0.5)
    elif act == "fgate":
        lb = _hg_lower_bound(lb_ref[...], layer)
        fg = lb + (1.0 - lb) * _sigmoid(acc)
        store(C_HL, jnp.log2(fg))
        acc = 1.0 - fg
    store(c0, acc)


def _in_proj_kernel(x_ref, g_ref, w_ref, lb_ref, o_ref, *, layer):
    h = _bf(_rms(x_ref[...], g_ref[...]))

    def store(c0, val):
        o_ref[:, c0:c0 + val.shape[1]] = val

    for group in _PROJ_GROUPS:
        _proj_group(h, w_ref, lb_ref, layer, group, store)


def _in_proj(x2d, gain, w, hg_lb, tm, layer):
    n = x2d.shape[0]
    c2 = lambda i: (0, 0)
    return pl.pallas_call(
        functools.partial(_in_proj_kernel, layer=layer),
        grid=(n // tm,),
        in_specs=[
            pl.BlockSpec((tm, D_MODEL), lambda i: (i, 0)),
            pl.BlockSpec((1, D_MODEL), c2),
            pl.BlockSpec((None, D_MODEL, W_COLS), lambda i: (layer, 0, 0), pipeline_mode=pl.Buffered(1)),
            pl.BlockSpec(hg_lb.shape, c2),
        ],
        out_specs=pl.BlockSpec((tm, PROJ_COLS), lambda i: (i, 0)),
        out_shape=jax.ShapeDtypeStruct((n, PROJ_COLS), jnp.float32),
        compiler_params=pltpu.CompilerParams(
            dimension_semantics=("parallel",), vmem_limit_bytes=VMEM_LIMIT),
        name="in_proj",
    )(x2d, gain, w, hg_lb)


FF_TILE = 256


def _out_ffn_kernel(x_ref, mix_ref, wo_ref, gn_ref, wi_ref, wf_ref, gf_ref, o_ref, *, final_norm):
    x1 = x_ref[...] + _dot(mix_ref[...], wo_ref[...])
    h2 = _bf(_rms(x1, gn_ref[...]))
    acc = x1
    for j in range(D_FF // FF_TILE):
        g = _dot(h2, wi_ref[:, j * FF_TILE:(j + 1) * FF_TILE])
        u = _dot(h2, wi_ref[:, D_FF + j * FF_TILE:D_FF + (j + 1) * FF_TILE])
        acc = acc + _dot(_bf(_silu(g) * u), wf_ref[j * FF_TILE:(j + 1) * FF_TILE, :])
    if final_norm:
        acc = _rms(acc, gf_ref[...])
    o_ref[...] = acc


def _out_ffn(x2d, mix2d, w_out, g_ffn, w_fi, w_fo, g_final, tm, layer, final_norm):
    n = x2d.shape[0]
    const = lambda i: (0, 0)
    lyr = lambda i: (layer, 0, 0)
    one = pl.Buffered(1)
    return pl.pallas_call(
        functools.partial(_out_ffn_kernel, final_norm=final_norm),
        grid=(n // tm,),
        in_specs=[
            pl.BlockSpec((tm, D_MODEL), lambda i: (i, 0)),
            pl.BlockSpec((tm, D_MIX), lambda i: (i, 0)),
            pl.BlockSpec((None, D_MIX, D_MODEL), lyr, pipeline_mode=one),
            pl.BlockSpec((1, D_MODEL), const),
            pl.BlockSpec((None, D_MODEL, 2 * D_FF), lyr, pipeline_mode=one),
            pl.BlockSpec((None, D_FF, D_MODEL), lyr, pipeline_mode=one),
            pl.BlockSpec((1, D_MODEL), const),
        ],
        out_specs=pl.BlockSpec((tm, D_MODEL), lambda i: (i, 0)),
        out_shape=jax.ShapeDtypeStruct((n, D_MODEL), jnp.float32),
        compiler_params=pltpu.CompilerParams(
            dimension_semantics=("parallel",), vmem_limit_bytes=VMEM_LIMIT),
        name="out_ffn",
    )(x2d, mix2d, w_out, g_ffn, w_fi, w_fo, g_final)


def _mixer_constants():
    L = CHUNK
    e3 = np.zeros((128, SSD_WIDTH), np.float32)
    for piece in range(3):
        for h in range(SSD_HEADS):
            e3[piece * SSD_HEADS + h, h * SSD_HEADDIM:(h + 1) * SSD_HEADDIM] = 1.0

    l = np.arange(L)[:, None]
    c = np.arange(SSD_WIDTH)[None, :]
    dsel = ((c % L) == l).astype(np.float32)
    s2 = np.arange(128)[None, :] % L
    causal2 = (s2 <= l).astype(np.float32)

    t = np.arange(L)
    blocks = []
    masks = []
    isq = np.zeros((L, 128), np.float32)
    for li, m in enumerate(HG_LEVELS):
        cm = np.zeros((L, L), np.float32)
        start = (t // (2 * m)) * (2 * m)
        mid = start + m - 1
        query = (t - start) >= m
        for s in range(L):
            if query[s]:
                cm[s, mid[s] + 1:s + 1] = 1.0
            else:
                cm[s, s + 1:mid[s] + 1] = 1.0
        blocks.append(cm)
        same = (start[:, None] == start[None, :])
        mk = same & query[:, None] & (~query[None, :])
        masks.append(np.concatenate([mk, mk], axis=1).astype(np.float32))
        isq[:, li] = query.astype(np.float32)
    tri = (t[None, :] <= t[:, None]).astype(np.float32)
    blocks.append(tri)
    blocks.append((t[None, :] > t[:, None]).astype(np.float32))
    ce = np.concatenate(blocks, axis=0)
    ce3 = np.concatenate([ce, ce, ce], axis=1)
    diag = (t[:, None] == t[None, :])
    masks.append(np.concatenate([diag, diag], axis=1).astype(np.float32))
    hmask = np.stack(masks, axis=0)
    return (jnp.asarray(e3, jnp.bfloat16), jnp.asarray(dsel), jnp.asarray(causal2),
            jnp.asarray(ce3, jnp.bfloat16), jnp.asarray(hmask), jnp.asarray(isq))


def _scan_time(x, op, identity, row):
    sh = 1
    while sh < x.shape[0]:
        r = pltpu.roll(x, sh, axis=0)
        x = op(x, jnp.where(row >= sh, r, identity))
        sh *= 2
    return x


def _blockdiag_rows(a, b):
    z = jnp.zeros_like(a)
    return jnp.concatenate([jnp.concatenate([a, z], axis=1), jnp.concatenate([z, b], axis=1)], axis=0)


def _mixer_chunk(col, put, sp_ref, cw_ref, cb_ref, dsk_ref, sg_ref, mlg_ref, hgg_ref,
                 e3_ref, dsel_ref, causal_ref, ce3_ref, hmask_ref, isq_ref,
                 mc_ref, mn_ref, mm_ref, ht_ref, hgt_ref, cbuf_ref, *, t_valid):
    L = CHUNK
    row = lax.broadcasted_iota(jnp.int32, (L, 128), 0)
    lane = lax.broadcasted_iota(jnp.int32, (L, 128), 1)
    lo_half = lane < 64
    causal2 = causal_ref[...] > 0.5

    nl = len(HG_LEVELS)
    ml_pairs = range(ML_HEADS // 2)
    hg_pairs = range(HG_HEADS // 2)

    sm = col(C_SM, 128)
    dt = _softplus(sm + sp_ref[0:1, :])
    a_neg2 = jnp.where(lane[0:1] < SSD_HEADS, -LOG2E * jnp.exp(sp_ref[1:2, :]), 0.0)
    ig = pltpu.roll(sm, LANE_MF - LANE_MI, axis=1) + sp_ref[2:3, :]
    lf = -_softplus(-(sm + sp_ref[3:4, :]))
    if t_valid < L:
        ok = row < t_valid
        dt = jnp.where(ok, dt, 0.0)
        ig = jnp.where(ok, ig, NEG_INF)
        lf = jnp.where(ok, lf, 0.0)
    ml_lane = (lane >= LANE_MF) & (lane < LANE_MF + ML_HEADS)
    cs = _scan_time(jnp.where(lane < SSD_HEADS, dt * a_neg2, jnp.where(ml_lane, lf, 0.0)), jnp.add, 0.0, row)
    acum2 = cs
    u = jnp.where(ml_lane, ig - cs, 0.0)
    m_prev = mm_ref[...]
    m_run = jnp.maximum(_scan_time(u, jnp.maximum, NEG_INF, row), m_prev)
    m_i = cs + m_run
    m_last = m_run[L - 1:L, :]
    w_inter = jnp.exp(m_prev - m_run)
    inv_floor = jnp.exp(-m_i)
    wk_s = jnp.exp(u - m_last)
    sc_s = jnp.exp(m_prev - m_last)

    hi, mid, lo = _split3(jnp.concatenate([dt, acum2], axis=0))
    lane2 = lax.broadcasted_iota(jnp.int32, (2 * L, 128), 1)
    packed = jnp.where(
        lane2 < SSD_HEADS, hi.astype(jnp.float32),
        jnp.where(lane2 < 2 * SSD_HEADS, pltpu.roll(mid.astype(jnp.float32), SSD_HEADS, axis=1),
                  jnp.where(lane2 < 3 * SSD_HEADS, pltpu.roll(lo.astype(jnp.float32), 2 * SSD_HEADS, axis=1), 0.0)))
    ex = _dot(_bf(packed), e3_ref[...])
    glog2 = col(C_HL, HG_WIDTH)
    if t_valid < L:
        glog2 = jnp.where(lax.broadcasted_iota(jnp.int32, (L, HG_WIDTH), 0) < t_valid, glog2, 0.0)
    ghi, gmid, glo = _split3(glog2)
    exps = _dot(ce3_ref[...], jnp.concatenate([ghi, gmid, glo], axis=0))
    yield

    xbc = col(C_XBC, CONV_CH)
    cbuf_ref[SUBLANES:SUBLANES + L, :] = xbc
    xbc_a = _silu(_causal_conv(cbuf_ref, cw_ref, cb_ref, L))
    cbuf_ref[CONV_TAIL:SUBLANES, :] = xbc[t_valid - (CONV_W - 1):t_valid, :]
    xs = xbc_a[:, 0:SSD_WIDTH]
    yield

    ml_q2 = [_bf(col(C_MQ + pp * 256, 256)) for pp in ml_pairs]
    ml_k = [col(C_MK + h * 128, 128) for h in range(ML_HEADS)]
    ml_v = [col(C_MV + h * 128, 128) for h in range(ML_HEADS)]
    ml_s2raw = [_dot_nt(ml_q2[pp], _bf(_blockdiag_rows(ml_k[2 * pp], ml_k[2 * pp + 1]))) for pp in ml_pairs]
    ml_qc = [_dot(ml_q2[h // 2][:, (h % 2) * 128:(h % 2 + 1) * 128], _bf(mc_ref[h])) for h in range(ML_HEADS)]

    kk = col(C_HK, HG_WIDTH)
    qq = col(C_HQ, HG_WIDTH)
    vv = col(C_HI, HG_WIDTH)
    qq_b, kk_b, vv_b = _bf(qq), _bf(kk), _bf(vv)
    hg_sl = [slice(h * 128, (h + 1) * 128) for h in range(HG_HEADS)]
    hg_adiag = [_dot_nt(qq_b[:, pp * 256:(pp + 1) * 256],
                        _blockdiag_rows(kk_b[:, hg_sl[2 * pp]], kk_b[:, hg_sl[2 * pp + 1]])) for pp in hg_pairs]

    ssd_bm, ssd_cb2, ssd_yi = [], [], []
    for g in range(SSD_GROUPS):
        bm = xbc_a[:, SSD_WIDTH + g * SSD_STATE:SSD_WIDTH + (g + 1) * SSD_STATE]
        cm = xbc_a[:, SSD_WIDTH + (SSD_GROUPS + g) * SSD_STATE:SSD_WIDTH + (SSD_GROUPS + g + 1) * SSD_STATE]
        bm_b, cm_b = _bf(bm), _bf(cm)
        ssd_bm.append(bm)
        ssd_cb2.append(_dot_nt(cm_b, jnp.concatenate([bm_b, bm_b], axis=0)))
        ssd_yi.append(_dot(cm_b, _bf(ht_ref[:, g * 512:(g + 1) * 512])))
    yield

    dt_f, ac_f = ex[0:L], ex[L:2 * L]
    eac_f = jnp.exp2(ac_f)
    dend_f = jnp.exp2(ac_f[L - 1:L, :] - ac_f)
    xt = xs * dt_f
    ctr = jnp.sum(jnp.where(dsel_ref[...] > 0.5, ac_f, 0.0), axis=0, keepdims=True)
    xdec = _bf(xt * dend_f)
    eac_end = eac_f[L - 1:L, :]
    yield

    qk_lv = []
    for li in range(nl):
        isq = isq_ref[:, li:li + 1] > 0.5
        qk_lv.append(_bf(jnp.where(isq, qq, kk) * jnp.exp2(exps[li * L:(li + 1) * L])))
    gc = exps[nl * L:(nl + 1) * L]
    gend = exps[(nl + 1) * L:(nl + 2) * L]
    q_dec = _bf(qq * jnp.exp2(gc))
    k_dec = _bf(kk * jnp.exp2(gend))
    s_dec = jnp.exp2(gc[L - 1:L, :])
    yield

    hg_masks = [hmask_ref[li] > 0.5 for li in range(nl + 1)]
    hg_a2 = []
    for pp in hg_pairs:
        a2 = jnp.where(hg_masks[nl], hg_adiag[pp], 0.0)
        for li in range(nl):
            x = qk_lv[li]
            a_l = _dot_nt(x[:, pp * 256:(pp + 1) * 256], _blockdiag_rows(x[:, hg_sl[2 * pp]], x[:, hg_sl[2 * pp + 1]]))
            a2 = jnp.where(hg_masks[li], a_l, a2)
        hg_a2.append(_bf(a2))
    hg_oi = [_dot_nt(q_dec[:, hg_sl[h]], _bf(hgt_ref[h])) for h in range(HG_HEADS)]
    yield

    dsel128 = dsel_ref[:, 0:128] > 0.5
    ml_s2, ml_dens = [], []
    for pp in ml_pairs:
        l0, l1 = LANE_MF + 2 * pp, LANE_MF + 2 * pp + 1
        m_row = jnp.where(lo_half, m_run[:, l0:l0 + 1], m_run[:, l1:l1 + 1])
        u_row = jnp.where(lo_half, u[:, l0:l0 + 1], u[:, l1:l1 + 1])
        u_col = jnp.sum(jnp.where(dsel128, u_row, 0.0), axis=0, keepdims=True)
        s2 = ml_s2raw[pp] * jnp.exp(jnp.where(causal2, u_col - m_row, NEG_INF))
        ml_s2.append(s2)
        ml_dens.append((jnp.sum(jnp.where(lo_half, s2, 0.0), axis=-1, keepdims=True),
                        jnp.sum(jnp.where(lo_half, 0.0, s2), axis=-1, keepdims=True)))
    ml_num2 = [_dot(_bf(ml_s2[pp]), _bf(_blockdiag_rows(ml_v[2 * pp], ml_v[2 * pp + 1]))) for pp in ml_pairs]
    yield

    y_parts = []
    for g in range(SSD_GROUPS):
        gs = slice(g * 512, (g + 1) * 512)
        for pp in range(4):
            ps = slice((g * 4 + pp) * 128, (g * 4 + pp + 1) * 128)
            seg = ac_f[:, ps] - ctr[:, ps]
            m2 = _bf(ssd_cb2[g] * jnp.exp2(jnp.where(causal2, seg, NEG_INF)))
            xp = xt[:, ps]
            xb = _bf(jnp.concatenate([jnp.where(lo_half, xp, 0.0), jnp.where(lo_half, 0.0, xp)], axis=0))
            y_parts.append(_dot(m2, xb) + eac_f[:, ps] * ssd_yi[g][:, pp * 128:(pp + 1) * 128])
        ht_ref[:, gs] = ht_ref[:, gs] * eac_end[:, gs] + _dot(_bf(ssd_bm[g].T), xdec[:, gs])
        yield

    hg_o2 = [_dot(hg_a2[pp], _blockdiag_rows(vv_b[:, hg_sl[2 * pp]], vv_b[:, hg_sl[2 * pp + 1]]))
             for pp in hg_pairs]

    for h in range(ML_HEADS):
        pp, j, ln = h // 2, h % 2, LANE_MF + h
        qh = col(C_MQ + h * 128, 128)
        wi = w_inter[:, ln:ln + 1]
        num = ml_num2[pp][:, j * 128:(j + 1) * 128] + wi * ml_qc[h]
        qn = jnp.sum(qh * mn_ref[h:h + 1, :], axis=-1, keepdims=True)
        den = ml_dens[pp][j] + wi * qn
        hval = num * (1.0 / jnp.maximum(jnp.abs(den), inv_floor[:, ln:ln + 1]))
        hn = _rms(hval, mlg_ref[:, h * 128:(h + 1) * 128])
        put(SSD_WIDTH + h * 128, _bf(col(C_MO + h * 128, 128) * hn))
        kw = ml_k[h] * wk_s[:, ln:ln + 1]
        sc = sc_s[:, ln:ln + 1]
        mc_ref[h] = sc * mc_ref[h] + _dot(_bf(kw.T), _bf(ml_v[h]))
        mn_ref[h:h + 1, :] = sc * mn_ref[h:h + 1, :] + jnp.sum(kw, axis=0, keepdims=True)
        if h % 2 == 1:
            yield
    mm_ref[...] = jnp.where(ml_lane[0:1], m_i[L - 1:L, :], 0.0)

    y = jnp.concatenate(y_parts, axis=1) + dsk_ref[...] * xs
    y_ssd = _rms(y * col(C_Z, SSD_WIDTH), sg_ref[...])
    put(0, _bf(y_ssd))
    yield

    for h in range(HG_HEADS):
        hs = hg_sl[h]
        o = hg_o2[h // 2][:, (h % 2) * 128:(h % 2 + 1) * 128] + hg_oi[h]
        on = _rms(o, hgg_ref[:, hs])
        put(SSD_WIDTH + ML_WIDTH + h * 128, _bf(on * col(C_HG + h * 128, 128)))
        hgt_ref[h] = hgt_ref[h] * s_dec[:, hs] + _dot(_bf(vv[:, hs].T), k_dec[:, hs])


def _lockstep(gens):
    gens = list(gens)
    while gens:
        alive = []
        for g in gens:
            try:
                next(g)
                alive.append(g)
            except StopIteration:
                pass
        gens = alive


def _mixer_kernel(*refs, t_rows, n_chunks, n_seq, layer, depth, has_state, has_prev, ring):
    it = iter(refs)
    proj_ref = next(it)
    state0 = [next(it) for _ in range(6)] if has_state else None
    sp_ref, cw_ref, cb_ref, dsk_ref, sg_ref, mlg_ref, hgg_ref = (next(it) for _ in range(7))
    e3_ref, dsel_ref, causal_ref, ce3_ref, hmask_ref, isq_ref = (next(it) for _ in range(6))
    if has_prev:
        for _ in range(6):
            next(it)
    mix_ref, conv_ref, ssd_ref, mc_ref, mn_ref, mm_ref, hg_ref = (next(it) for _ in range(7))
    cbuf_ref, ht_ref, hgt_ref = (next(it) for _ in range(3))
    padded = t_rows < CHUNK
    pbuf_ref = next(it) if padded else None
    if ring:
        pring_ref, ring_sem = next(it), next(it)
    tb = pl.program_id(1)

    if ring:
        nb = pl.num_programs(1)
        n_steps = pl.num_programs(0) * nb
        step_n = pl.program_id(0) * nb + tb

        def ring_copy(n):
            return pltpu.make_async_copy(
                proj_ref.at[n // nb, pl.ds((n % nb) * t_rows, t_rows), :],
                pring_ref.at[n % RING_SLOTS], ring_sem.at[n % RING_SLOTS])

        @pl.when(step_n == 0)
        def _prime():
            for k in range(RING_SLOTS - 1):
                ring_copy(jnp.int32(k)).start()

        @pl.when(step_n + RING_SLOTS - 1 < n_steps)
        def _ahead():
            ring_copy(step_n + RING_SLOTS - 1).start()

        ring_copy(step_n).wait()
        slot = step_n % RING_SLOTS
    li = 0 if has_prev else layer
    out_refs = (conv_ref, ssd_ref, mc_ref, mn_ref, mm_ref, hg_ref)

    @pl.when(tb == 0)
    def _init():
        if not has_prev:
            for other in range(depth):
                if other != layer:
                    for r in out_refs:
                        r[other] = jnp.zeros(r.shape[1:], jnp.float32)
        for s in range(n_seq):
            if has_state:
                conv0_ref, ssd0_ref, mc0_ref, mn0_ref, mm0_ref, hg0_ref = state0
                cbuf_ref[s, CONV_TAIL:SUBLANES, :] = conv0_ref[0, s]
                ht_ref[s] = ssd0_ref[0, s].reshape(SSD_WIDTH, SSD_STATE).T
                mc_ref[li, s] = mc0_ref[0, s]
                mn_ref[li, s] = mn0_ref[0, s]
                mm_ref[li, s] = mm0_ref[0, s]
                for h in range(HG_HEADS):
                    hgt_ref[s, h] = hg0_ref[0, s, h].T
            else:
                cbuf_ref[s, CONV_TAIL:SUBLANES, :] = jnp.zeros((CONV_W - 1, CONV_CH), jnp.float32)
                ht_ref[s] = jnp.zeros(ht_ref.shape[1:], jnp.float32)
                mc_ref[li, s] = jnp.zeros(mc_ref.shape[2:], jnp.float32)
                mn_ref[li, s] = jnp.zeros(mn_ref.shape[2:], jnp.float32)
                mm_ref[li, s] = jnp.zeros(mm_ref.shape[2:], jnp.float32)
                hgt_ref[s] = jnp.zeros(hgt_ref.shape[1:], jnp.float32)

    def chunk(s, col, put):
        return _mixer_chunk(
            col, put, sp_ref=sp_ref, cw_ref=cw_ref, cb_ref=cb_ref, dsk_ref=dsk_ref, sg_ref=sg_ref, mlg_ref=mlg_ref,
            hgg_ref=hgg_ref, e3_ref=e3_ref, dsel_ref=dsel_ref, causal_ref=causal_ref, ce3_ref=ce3_ref,
            hmask_ref=hmask_ref, isq_ref=isq_ref, mc_ref=mc_ref.at[li, s], mn_ref=mn_ref.at[li, s],
            mm_ref=mm_ref.at[li, s], ht_ref=ht_ref.at[s], hgt_ref=hgt_ref.at[s], cbuf_ref=cbuf_ref.at[s],
            t_valid=t_rows if padded else CHUNK)

    if padded:
        gens = []
        for s in range(n_seq):
            pbuf_ref[s, 0:t_rows, :] = proj_ref[s]
            pbuf_ref[s, t_rows:CHUNK, :] = jnp.zeros((CHUNK - t_rows, PROJ_COLS), jnp.float32)

            def col(c0, width, s=s):
                return pbuf_ref[s, :, c0:c0 + width]

            def put(c0, val, s=s):
                mix_ref[s, :, c0:c0 + val.shape[1]] = val[0:t_rows]

            gens.append(chunk(s, col, put))
        _lockstep(gens)
    else:
        def body(ci, carry):
            rows = pl.ds(pl.multiple_of(ci * CHUNK, CHUNK), CHUNK)
            gens = []
            for s in range(n_seq):
                def col(c0, width, s=s):
                    if ring:
                        return pring_ref[slot, rows, c0:c0 + width]
                    return proj_ref[s, rows, c0:c0 + width]

                def put(c0, val, s=s):
                    mix_ref[s, rows, c0:c0 + val.shape[1]] = val

                gens.append(chunk(s, col, put))
            _lockstep(gens)
            return carry
        lax.fori_loop(0, n_chunks, body, 0, unroll=2 if n_chunks % 2 == 0 else 1)

    @pl.when(tb == pl.num_programs(1) - 1)
    def _fin():
        for s in range(n_seq):
            conv_ref[li, s] = cbuf_ref[s, CONV_TAIL:SUBLANES, :]
            ssd_ref[li, s] = ht_ref[s].T.reshape(SSD_HEADS, SSD_HEADDIM, SSD_STATE)
            for h in range(HG_HEADS):
                hg_ref[li, s, h] = hgt_ref[s, h].T


def _mixer(proj, state_in, prev, params, consts, layer, depth):
    b, t, _ = proj.shape
    if t < CHUNK:
        tblk = t
    else:
        tblk = min(TIME_BLOCK, t)
        assert t % tblk == 0 and tblk % CHUNK == 0
    ns = SEQ_PER_STEP
    assert b % ns == 0
    c2 = lambda i, j: (0, 0)
    c3 = lambda i, j: (0, 0, 0)
    state_dims = [(CONV_W - 1, CONV_CH), (SSD_HEADS, SSD_HEADDIM, SSD_STATE), (ML_HEADS, ML_HD, ML_HD),
                  (ML_HEADS, ML_HD), (1, 128), (HG_HEADS, HG_HD, HG_HD)]

    def state_spec(dims, layers, first):
        return pl.BlockSpec((layers, ns) + dims, lambda i, j: (first, i) + (0,) * len(dims))

    in_state_specs = [state_spec(d, 1, layer) for d in state_dims]
    out_state_specs = [state_spec(d, 1, layer) if prev is not None else state_spec(d, depth, 0) for d in state_dims]
    state_shapes = [
        jax.ShapeDtypeStruct((depth, b, CONV_W - 1, CONV_CH), jnp.float32),
        jax.ShapeDtypeStruct((depth, b, SSD_HEADS, SSD_HEADDIM, SSD_STATE), jnp.float32),
        jax.ShapeDtypeStruct((depth, b, ML_HEADS, ML_HD, ML_HD), jnp.float32),
        jax.ShapeDtypeStruct((depth, b, ML_HEADS, ML_HD), jnp.float32),
        jax.ShapeDtypeStruct((depth, b, 1, 128), jnp.float32),
        jax.ShapeDtypeStruct((depth, b, HG_HEADS, HG_HD, HG_HD), jnp.float32),
    ]
    ring = t >= CHUNK and ns == 1 and b * (t // tblk) >= RING_SLOTS - 1
    args = [proj]
    in_specs = [pl.BlockSpec(memory_space=pl.ANY) if ring
                else pl.BlockSpec((ns, tblk, PROJ_COLS), lambda i, j: (i, j, 0))]
    if state_in is not None:
        args += list(state_in)
        in_specs += in_state_specs
    args += list(params) + list(consts)
    in_specs += [pl.BlockSpec(a.shape, c3 if a.ndim == 3 else c2) for a in list(params) + list(consts)]
    aliases = {}
    if prev is not None:
        for k, a in enumerate(prev):
            aliases[len(args)] = 1 + k
            args.append(a)
            in_specs.append(pl.BlockSpec(memory_space=pl.ANY))
    scratch = [pltpu.VMEM((ns, SUBLANES + CHUNK, CONV_CH), jnp.float32),
               pltpu.VMEM((ns, SSD_STATE, SSD_WIDTH), jnp.float32),
               pltpu.VMEM((ns, HG_HEADS, HG_HD, HG_HD), jnp.float32)]
    if t < CHUNK:
        scratch.append(pltpu.VMEM((ns, CHUNK, PROJ_COLS), jnp.float32))
    if ring:
        scratch += [pltpu.VMEM((RING_SLOTS, tblk, PROJ_COLS), jnp.float32), pltpu.SemaphoreType.DMA((RING_SLOTS,))]
    return pl.pallas_call(
        functools.partial(_mixer_kernel, t_rows=tblk, n_chunks=max(tblk // CHUNK, 1), n_seq=ns, layer=layer,
                          depth=depth, has_state=state_in is not None, has_prev=prev is not None, ring=ring),
        grid=(b // ns, t // tblk),
        in_specs=in_specs,
        out_specs=[pl.BlockSpec((ns, tblk, D_MIX), lambda i, j: (i, j, 0))] + out_state_specs,
        out_shape=[jax.ShapeDtypeStruct((b, t, D_MIX), jnp.bfloat16)] + state_shapes,
        scratch_shapes=scratch,
        input_output_aliases=aliases,
        compiler_params=pltpu.CompilerParams(
            dimension_semantics=("arbitrary" if ring else "parallel", "arbitrary"), vmem_limit_bytes=VMEM_LIMIT),
        name="mixer",
    )(*args)


_W_IN_GROUPS = (
    (1024, C_Z), (1536, C_XBC), (16, C_SM + LANE_DT), (512, C_MQ), (512, C_MK), (512, C_MV),
    (4, C_SM + LANE_MI), (4, C_SM + LANE_MF), (512, C_MO), (512, C_HQ), (512, C_HK), (512, C_HI), (512, C_HG),
)
W_IN_COLS = sum(width for width, _ in _W_IN_GROUPS)
REGROUP_ROWS = 128


def _regroup_kernel(w_ref, o_ref):
    o_ref[:, C_SM:C_SM + 128] = jnp.zeros((REGROUP_ROWS, 128), jnp.bfloat16)
    src = 0
    for width, dst in _W_IN_GROUPS:
        o_ref[:, dst:dst + width] = w_ref[:, src:src + width].astype(jnp.bfloat16)
        src += width


def _regroup_w_in(w):
    depth = w.shape[0]
    return pl.pallas_call(
        _regroup_kernel,
        grid=(depth, D_MODEL // REGROUP_ROWS),
        in_specs=[pl.BlockSpec((None, REGROUP_ROWS, W_IN_COLS), lambda l, i: (l, i, 0))],
        out_specs=pl.BlockSpec((None, REGROUP_ROWS, W_COLS), lambda l, i: (l, i, 0)),
        out_shape=jax.ShapeDtypeStruct((depth, D_MODEL, W_COLS), jnp.bfloat16),
        compiler_params=pltpu.CompilerParams(
            dimension_semantics=("parallel", "parallel"), vmem_limit_bytes=VMEM_LIMIT),
        name="regroup_w_in",
    )(w)


def _small_params(dt_bias, a_log, ml_bi, ml_bf):
    sp = jnp.zeros((8, 128), jnp.float32)
    sp = sp.at[0, LANE_DT:LANE_DT + SSD_HEADS].set(dt_bias)
    sp = sp.at[1, LANE_DT:LANE_DT + SSD_HEADS].set(a_log)
    sp = sp.at[2, LANE_MF:LANE_MF + ML_HEADS].set(ml_bi)
    sp = sp.at[3, LANE_MF:LANE_MF + ML_HEADS].set(ml_bf)
    return sp


def _run_trunk(x, state_in, lp, norm_final, consts, tm_proj, tm_ffn):
    b, t, _ = x.shape
    assert t % CHUNK == 0 or t < CHUNK, "a partial chunk is only supported for single-chunk sequences"
    depth = len(lp)
    x2d = x.reshape(b * t, D_MODEL)
    states = None
    for l in range(depth):
        p = lp[l]
        params = (p["sp"], p["conv_w"], p["conv_b"], p["d_skip"], p["ssd_gain"], p["ml_gain"], p["hg_gain"])
        proj = _in_proj(x2d, p["norm_mix"], p["w_in"], p["hg_lb"], tm_proj, l)
        res = _mixer(proj.reshape(b, t, PROJ_COLS), state_in, states, params, consts, l, depth)
        states = res[1:]
        x2d = _out_ffn(x2d, res[0].reshape(b * t, D_MIX), p["w_out"], p["norm_ffn"], p["w_ffn_in"],
                       p["w_ffn_out"], norm_final, tm_ffn, l, l == depth - 1)
    conv, ssd, mc, mn, mm_p, hg = states
    return x2d.reshape(b, t, D_MODEL), (conv, ssd, mc, mn, mm_p[:, :, 0, LANE_MF:LANE_MF + ML_HEADS], hg)


def kernel(x_prompt, x_sample, state_conv, state_ssd, state_mlstm_c, state_mlstm_n, state_mlstm_m, state_hgrn,
           norm_mix, w_in, conv_w, conv_b, dt_bias, a_log, d_skip, ssd_gain, ml_bi, ml_bf, ml_gain,
           hg_lb, hg_gain, w_out, norm_ffn, w_ffn_in, w_ffn_out, norm_final):
    depth = w_in.shape[0]
    f32 = jnp.float32
    consts = _mixer_constants()
    bf16 = jnp.bfloat16
    w_in_b, w_out_b, w_fi_b, w_fo_b = (_regroup_w_in(w_in), w_out.astype(bf16), w_ffn_in.astype(bf16),
                                       w_ffn_out.astype(bf16))
    lp = []
    for l in range(depth):
        lp.append({
            "norm_mix": norm_mix[l].reshape(1, D_MODEL),
            "w_in": w_in_b,
            "sp": _small_params(dt_bias[l], a_log[l], ml_bi[l], ml_bf[l]),
            "conv_w": conv_w[l],
            "conv_b": conv_b[l].reshape(1, CONV_CH),
            "d_skip": jnp.repeat(d_skip[l], SSD_HEADDIM).reshape(1, SSD_WIDTH),
            "ssd_gain": ssd_gain[l].reshape(1, SSD_WIDTH),
            "ml_gain": ml_gain[l].reshape(1, ML_WIDTH),
            "hg_lb": hg_lb.astype(f32),
            "hg_gain": hg_gain[l].reshape(1, HG_WIDTH),
            "w_out": w_out_b,
            "norm_ffn": norm_ffn[l].reshape(1, D_MODEL),
            "w_ffn_in": w_fi_b,
            "w_ffn_out": w_fo_b,
        })
    nf = norm_final.reshape(1, D_MODEL)

    y_prompt, p_outs = _run_trunk(x_prompt, None, lp, nf, consts, 256, 512)
    bs = x_sample.shape[0]
    mm_p = jnp.pad(state_mlstm_m.astype(f32)[:, :, None, :],
                   ((0, 0), (0, 0), (0, 0), (LANE_MF, 128 - LANE_MF - ML_HEADS)))
    s_in = (state_conv, state_ssd, state_mlstm_c, state_mlstm_n, mm_p, state_hgrn)
    y_sample, s_outs = _run_trunk(x_sample, s_in, lp, nf, consts, 256, 512)
    return (y_prompt, y_sample) + p_outs + s_outs
```

```python
import functools

import numpy as np
import jax
import jax.numpy as jnp
from jax import lax
from jax.experimental import pallas as pl
from jax.experimental.pallas import tpu as pltpu

D_MODEL = 1024
CHUNK = 64
SSD_WIDTH = 1024
SSD_HEADDIM = 64
SSD_HEADS = 16
SSD_GROUPS = 2
SSD_STATE = 128
CONV_W = 4
CONV_CH = SSD_WIDTH + 2 * SSD_GROUPS * SSD_STATE
ML_WIDTH = 512
ML_HEADS = 4
ML_HD = 128
HG_WIDTH = 512
HG_HEADS = 4
HG_HD = 128
D_MIX = 2048
D_FF = 2816
EPS = 1e-6

C_Z = 0
C_XBC = 1024
C_MQ, C_MK, C_MV, C_MO = 2560, 3072, 3584, 4096
C_HQ, C_HK, C_HI, C_HG = 4608, 5120, 5632, 6144
C_SM = 6656
W_COLS = 6784
C_HL = 6784
PROJ_COLS = 7296
TIME_BLOCK = 256
SEQ_PER_STEP = 1
RING_SLOTS = 3
LOG2E = 1.4426950408889634
LANE_DT = 0
LANE_MI = 16
LANE_MF = 20

VMEM_LIMIT = 52 * 1024 * 1024
SUBLANES = 8
CONV_TAIL = SUBLANES - (CONV_W - 1)
NEG_INF = float("-inf")

HG_LEVELS = (32, 16, 8, 4, 2, 1)

_NT = (((1,), (1,)), ((), ()))


def _bf(x):
    return x.astype(jnp.bfloat16)


def _dot(a, b):
    return jnp.dot(a, b, preferred_element_type=jnp.float32)


def _dot_nt(a, b):
    return lax.dot_general(a, b, _NT, preferred_element_type=jnp.float32)


def _sigmoid(x):
    return 1.0 / (1.0 + jnp.exp(-x))


def _silu(x):
    return x * _sigmoid(x)


def _softplus(x):
    return jnp.maximum(x, 0.0) + jnp.log(1.0 + jnp.exp(-jnp.abs(x)))


def _split3(x):
    hi = _bf(x)
    r1 = x - hi.astype(jnp.float32)
    mid = _bf(r1)
    r2 = r1 - mid.astype(jnp.float32)
    return hi, mid, _bf(r2)


def _rms(x, gain):
    return x * lax.rsqrt(jnp.mean(x * x, axis=-1, keepdims=True) + EPS) * gain


def _hg_lower_bound(lb_all, layer):
    lb_e = jnp.exp(lb_all - jnp.max(lb_all, axis=0, keepdims=True))
    lb_soft = lb_e * (1.0 / jnp.sum(lb_e, axis=0, keepdims=True))
    return jnp.sum(lb_soft[0:layer + 1], axis=0, keepdims=True) - lb_soft[0:1]


def _causal_conv(cbuf_ref, cw_ref, cb_ref, rows):
    acc = cb_ref[...] + cbuf_ref[SUBLANES:SUBLANES + rows, :] * cw_ref[CONV_W - 1:CONV_W, :]
    for w in range(CONV_W - 1):
        acc = acc + cbuf_ref[CONV_TAIL + w:CONV_TAIL + w + rows, :] * cw_ref[w:w + 1, :]
    return acc


_PROJ_GROUPS = (
    (C_Z, 512, "silu"), (C_Z + 512, 512, "silu"),
    (C_XBC, 512, None), (C_XBC + 512, 512, None), (C_XBC + 1024, 512, None),
    (C_MQ, 512, None), (C_MK, 512, "kscale"), (C_MV, 512, None), (C_MO, 512, "sigmoid"),
    (C_HQ, 512, "silu"), (C_HK, 512, "fgate"), (C_HI, 512, None), (C_HG, 512, "silu"),
    (C_SM, 128, None),
)


def _proj_group(h, w_ref, lb_ref, layer, group, store):
    c0, width, act = group
    acc = _dot(h, w_ref[:, c0:c0 + width])
    if act == "silu":
        acc = _silu(acc)
    elif act == "sigmoid":
        acc = _sigmoid(acc)
    elif act == "kscale":
        acc = acc * (ML_HD ** -0.5)
    elif act == "fgate":
        lb = _hg_lower_bound(lb_ref[...], layer)
        fg = lb + (1.0 - lb) * _sigmoid(acc)
        store(C_HL, jnp.log2(fg))
        acc = 1.0 - fg
    store(c0, acc)


def _in_proj_kernel(x_ref, g_ref, w_ref, lb_ref, o_ref, xring_ref, ring_sem, *, layer):
    step, n_steps, tm = pl.program_id(0), pl.num_programs(0), xring_ref.shape[1]

    def ring_copy(n):
        return pltpu.make_async_copy(x_ref.at[pl.ds(n * tm, tm), :], xring_ref.at[n % RING_SLOTS],
                                     ring_sem.at[n % RING_SLOTS])

    @pl.when(step == 0)
    def _prime():
        for k in range(RING_SLOTS - 1):
            ring_copy(jnp.int32(k)).start()

    @pl.when(step + RING_SLOTS - 1 < n_steps)
    def _ahead():
        ring_copy(step + RING_SLOTS - 1).start()

    ring_copy(step).wait()
    h = _bf(_rms(xring_ref[step % RING_SLOTS], g_ref[...]))

    def store(c0, val):
        o_ref[:, c0:c0 + val.shape[1]] = val

    for group in _PROJ_GROUPS:
        _proj_group(h, w_ref, lb_ref, layer, group, store)


def _in_proj(x2d, gain, w, hg_lb, tm, layer):
    n = x2d.shape[0]
    assert n % tm == 0 and n // tm >= RING_SLOTS - 1
    c2 = lambda i: (0, 0)
    return pl.pallas_call(
        functools.partial(_in_proj_kernel, layer=layer),
        grid=(n // tm,),
        in_specs=[
            pl.BlockSpec(memory_space=pl.ANY),
            pl.BlockSpec((1, D_MODEL), c2),
            pl.BlockSpec((None, D_MODEL, W_COLS), lambda i: (layer, 0, 0), pipeline_mode=pl.Buffered(1)),
            pl.BlockSpec(hg_lb.shape, c2),
        ],
        out_specs=pl.BlockSpec((tm, PROJ_COLS), lambda i: (i, 0)),
        out_shape=jax.ShapeDtypeStruct((n, PROJ_COLS), jnp.float32),
        scratch_shapes=[pltpu.VMEM((RING_SLOTS, tm, D_MODEL), jnp.float32), pltpu.SemaphoreType.DMA((RING_SLOTS,))],
        compiler_params=pltpu.CompilerParams(
            dimension_semantics=("arbitrary",), vmem_limit_bytes=VMEM_LIMIT),
        name="in_proj",
    )(x2d, gain, w, hg_lb)


FF_TILE = 256


def _out_ffn_kernel(x_ref, mix_ref, wo_ref, gn_ref, wi_ref, wf_ref, gf_ref, o_ref, *, final_norm):
    x1 = x_ref[...] + _dot(mix_ref[...], wo_ref[...])
    h2 = _bf(_rms(x1, gn_ref[...]))
    acc = x1
    for j in range(D_FF // FF_TILE):
        g = _dot(h2, wi_ref[:, j * FF_TILE:(j + 1) * FF_TILE])
        u = _dot(h2, wi_ref[:, D_FF + j * FF_TILE:D_FF + (j + 1) * FF_TILE])
        acc = acc + _dot(_bf(_silu(g) * u), wf_ref[j * FF_TILE:(j + 1) * FF_TILE, :])
    if final_norm:
        acc = _rms(acc, gf_ref[...])
    o_ref[...] = acc


def _out_ffn(x2d, mix2d, w_out, g_ffn, w_fi, w_fo, g_final, tm, layer, final_norm):
    n = x2d.shape[0]
    const = lambda i: (0, 0)
    lyr = lambda i: (layer, 0, 0)
    one = pl.Buffered(1)
    return pl.pallas_call(
        functools.partial(_out_ffn_kernel, final_norm=final_norm),
        grid=(n // tm,),
        in_specs=[
            pl.BlockSpec((tm, D_MODEL), lambda i: (i, 0)),
            pl.BlockSpec((tm, D_MIX), lambda i: (i, 0)),
            pl.BlockSpec((None, D_MIX, D_MODEL), lyr, pipeline_mode=one),
            pl.BlockSpec((1, D_MODEL), const),
            pl.BlockSpec((None, D_MODEL, 2 * D_FF), lyr, pipeline_mode=one),
            pl.BlockSpec((None, D_FF, D_MODEL), lyr, pipeline_mode=one),
            pl.BlockSpec((1, D_MODEL), const),
        ],
        out_specs=pl.BlockSpec((tm, D_MODEL), lambda i: (i, 0)),
        out_shape=jax.ShapeDtypeStruct((n, D_MODEL), jnp.float32),
        compiler_params=pltpu.CompilerParams(
            dimension_semantics=("parallel",), vmem_limit_bytes=VMEM_LIMIT),
        name="out_ffn",
    )(x2d, mix2d, w_out, g_ffn, w_fi, w_fo, g_final)


def _mixer_constants():
    L = CHUNK
    e3 = np.zeros((128, SSD_WIDTH), np.float32)
    for piece in range(3):
        for h in range(SSD_HEADS):
            e3[piece * SSD_HEADS + h, h * SSD_HEADDIM:(h + 1) * SSD_HEADDIM] = 1.0

    l = np.arange(L)[:, None]
    c = np.arange(SSD_WIDTH)[None, :]
    dsel = ((c % L) == l).astype(np.float32)
    s2 = np.arange(128)[None, :] % L
    causal2 = (s2 <= l).astype(np.float32)

    t = np.arange(L)
    blocks = []
    masks = []
    isq = np.zeros((L, 128), np.float32)
    for li, m in enumerate(HG_LEVELS):
        cm = np.zeros((L, L), np.float32)
        start = (t // (2 * m)) * (2 * m)
        mid = start + m - 1
        query = (t - start) >= m
        for s in range(L):
            if query[s]:
                cm[s, mid[s] + 1:s + 1] = 1.0
            else:
                cm[s, s + 1:mid[s] + 1] = 1.0
        blocks.append(cm)
        same = (start[:, None] == start[None, :])
        mk = same & query[:, None] & (~query[None, :])
        masks.append(np.concatenate([mk, mk], axis=1).astype(np.float32))
        isq[:, li] = query.astype(np.float32)
    tri = (t[None, :] <= t[:, None]).astype(np.float32)
    blocks.append(tri)
    blocks.append((t[None, :] > t[:, None]).astype(np.float32))
    ce = np.concatenate(blocks, axis=0)
    ce3 = np.concatenate([ce, ce, ce], axis=1)
    diag = (t[:, None] == t[None, :])
    masks.append(np.concatenate([diag, diag], axis=1).astype(np.float32))
    hmask = np.stack(masks, axis=0)
    return (jnp.asarray(e3, jnp.bfloat16), jnp.asarray(dsel), jnp.asarray(causal2),
            jnp.asarray(ce3, jnp.bfloat16), jnp.asarray(hmask), jnp.asarray(isq))


def _scan_time(x, op, identity, row):
    sh = 1
    while sh < x.shape[0]:
        r = pltpu.roll(x, sh, axis=0)
        x = op(x, jnp.where(row >= sh, r, identity))
        sh *= 2
    return x


def _blockdiag_rows(a, b):
    z = jnp.zeros_like(a)
    return jnp.concatenate([jnp.concatenate([a, z], axis=1), jnp.concatenate([z, b], axis=1)], axis=0)


def _mixer_chunk(col, put, sp_ref, cw_ref, cb_ref, dsk_ref, sg_ref, mlg_ref, hgg_ref,
                 e3_ref, dsel_ref, causal_ref, ce3_ref, hmask_ref, isq_ref,
                 mc_ref, mn_ref, mm_ref, ht_ref, hgt_ref, cbuf_ref, *, t_valid):
    L = CHUNK
    row = lax.broadcasted_iota(jnp.int32, (L, 128), 0)
    lane = lax.broadcasted_iota(jnp.int32, (L, 128), 1)
    lo_half = lane < 64
    causal2 = causal_ref[...] > 0.5

    nl = len(HG_LEVELS)
    ml_pairs = range(ML_HEADS // 2)
    hg_pairs = range(HG_HEADS // 2)

    sm = col(C_SM, 128)
    dt = _softplus(sm + sp_ref[0:1, :])
    a_neg2 = jnp.where(lane[0:1] < SSD_HEADS, -LOG2E * jnp.exp(sp_ref[1:2, :]), 0.0)
    ig = pltpu.roll(sm, LANE_MF - LANE_MI, axis=1) + sp_ref[2:3, :]
    lf = -_softplus(-(sm + sp_ref[3:4, :]))
    if t_valid < L:
        ok = row < t_valid
        dt = jnp.where(ok, dt, 0.0)
        ig = jnp.where(ok, ig, NEG_INF)
        lf = jnp.where(ok, lf, 0.0)
    ml_lane = (lane >= LANE_MF) & (lane < LANE_MF + ML_HEADS)
    cs = _scan_time(jnp.where(lane < SSD_HEADS, dt * a_neg2, jnp.where(ml_lane, lf, 0.0)), jnp.add, 0.0, row)
    acum2 = cs
    u = jnp.where(ml_lane, ig - cs, 0.0)
    m_prev = mm_ref[...]
    m_run = jnp.maximum(_scan_time(u, jnp.maximum, NEG_INF, row), m_prev)
    m_i = cs + m_run
    m_last = m_run[L - 1:L, :]
    w_inter = jnp.exp(m_prev - m_run)
    inv_floor = jnp.exp(-m_i)
    wk_s = jnp.exp(u - m_last)
    sc_s = jnp.exp(m_prev - m_last)

    hi, mid, lo = _split3(jnp.concatenate([dt, acum2], axis=0))
    lane2 = lax.broadcasted_iota(jnp.int32, (2 * L, 128), 1)
    packed = jnp.where(
        lane2 < SSD_HEADS, hi.astype(jnp.float32),
        jnp.where(lane2 < 2 * SSD_HEADS, pltpu.roll(mid.astype(jnp.float32), SSD_HEADS, axis=1),
                  jnp.where(lane2 < 3 * SSD_HEADS, pltpu.roll(lo.astype(jnp.float32), 2 * SSD_HEADS, axis=1), 0.0)))
    ex = _dot(_bf(packed), e3_ref[...])
    glog2 = col(C_HL, HG_WIDTH)
    if t_valid < L:
        glog2 = jnp.where(lax.broadcasted_iota(jnp.int32, (L, HG_WIDTH), 0) < t_valid, glog2, 0.0)
    ghi, gmid, glo = _split3(glog2)
    exps = _dot(ce3_ref[...], jnp.concatenate([ghi, gmid, glo], axis=0))
    yield

    xbc = col(C_XBC, CONV_CH)
    cbuf_ref[SUBLANES:SUBLANES + L, :] = xbc
    xbc_a = _silu(_causal_conv(cbuf_ref, cw_ref, cb_ref, L))
    cbuf_ref[CONV_TAIL:SUBLANES, :] = xbc[t_valid - (CONV_W - 1):t_valid, :]
    xs = xbc_a[:, 0:SSD_WIDTH]
    yield

    ml_q2 = [_bf(col(C_MQ + pp * 256, 256)) for pp in ml_pairs]
    ml_k = [col(C_MK + h * 128, 128) for h in range(ML_HEADS)]
    ml_v = [col(C_MV + h * 128, 128) for h in range(ML_HEADS)]
    ml_s2raw = [_dot_nt(ml_q2[pp], _bf(_blockdiag_rows(ml_k[2 * pp], ml_k[2 * pp + 1]))) for pp in ml_pairs]
    ml_qc = [_dot(ml_q2[h // 2][:, (h % 2) * 128:(h % 2 + 1) * 128], _bf(mc_ref[h])) for h in range(ML_HEADS)]

    kk = col(C_HK, HG_WIDTH)
    qq = col(C_HQ, HG_WIDTH)
    vv = col(C_HI, HG_WIDTH)
    qq_b, kk_b, vv_b = _bf(qq), _bf(kk), _bf(vv)
    hg_sl = [slice(h * 128, (h + 1) * 128) for h in range(HG_HEADS)]
    hg_adiag = [_dot_nt(qq_b[:, pp * 256:(pp + 1) * 256],
                        _blockdiag_rows(kk_b[:, hg_sl[2 * pp]], kk_b[:, hg_sl[2 * pp + 1]])) for pp in hg_pairs]

    ssd_bm, ssd_cb2, ssd_yi = [], [], []
    for g in range(SSD_GROUPS):
        bm = xbc_a[:, SSD_WIDTH + g * SSD_STATE:SSD_WIDTH + (g + 1) * SSD_STATE]
        cm = xbc_a[:, SSD_WIDTH + (SSD_GROUPS + g) * SSD_STATE:SSD_WIDTH + (SSD_GROUPS + g + 1) * SSD_STATE]
        bm_b, cm_b = _bf(bm), _bf(cm)
        ssd_bm.append(bm)
        ssd_cb2.append(_dot_nt(cm_b, jnp.concatenate([bm_b, bm_b], axis=0)))
        ssd_yi.append(_dot(cm_b, _bf(ht_ref[:, g * 512:(g + 1) * 512])))
    yield

    dt_f, ac_f = ex[0:L], ex[L:2 * L]
    eac_f = jnp.exp2(ac_f)
    dend_f = jnp.exp2(ac_f[L - 1:L, :] - ac_f)
    xt = xs * dt_f
    ctr = jnp.sum(jnp.where(dsel_ref[...] > 0.5, ac_f, 0.0), axis=0, keepdims=True)
    xdec = _bf(xt * dend_f)
    eac_end = eac_f[L - 1:L, :]
    yield

    qk_lv = []
    for li in range(nl):
        isq = isq_ref[:, li:li + 1] > 0.5
        qk_lv.append(_bf(jnp.where(isq, qq, kk) * jnp.exp2(exps[li * L:(li + 1) * L])))
    gc = exps[nl * L:(nl + 1) * L]
    gend = exps[(nl + 1) * L:(nl + 2) * L]
    q_dec = _bf(qq * jnp.exp2(gc))
    k_dec = _bf(kk * jnp.exp2(gend))
    s_dec = jnp.exp2(gc[L - 1:L, :])
    yield

    hg_masks = [hmask_ref[li] > 0.5 for li in range(nl + 1)]
    hg_a2 = []
    for pp in hg_pairs:
        a2 = jnp.where(hg_masks[nl], hg_adiag[pp], 0.0)
        for li in range(nl):
            x = qk_lv[li]
            a_l = _dot_nt(x[:, pp * 256:(pp + 1) * 256], _blockdiag_rows(x[:, hg_sl[2 * pp]], x[:, hg_sl[2 * pp + 1]]))
            a2 = jnp.where(hg_masks[li], a_l, a2)
        hg_a2.append(_bf(a2))
    hg_oi = [_dot_nt(q_dec[:, hg_sl[h]], _bf(hgt_ref[h])) for h in range(HG_HEADS)]
    yield

    dsel128 = dsel_ref[:, 0:128] > 0.5
    ml_s2, ml_dens = [], []
    for pp in ml_pairs:
        l0, l1 = LANE_MF + 2 * pp, LANE_MF + 2 * pp + 1
        m_row = jnp.where(lo_half, m_run[:, l0:l0 + 1], m_run[:, l1:l1 + 1])
        u_row = jnp.where(lo_half, u[:, l0:l0 + 1], u[:, l1:l1 + 1])
        u_col = jnp.sum(jnp.where(dsel128, u_row, 0.0), axis=0, keepdims=True)
        s2 = ml_s2raw[pp] * jnp.exp(jnp.where(causal2, u_col - m_row, NEG_INF))
        ml_s2.append(s2)
        ml_dens.append((jnp.sum(jnp.where(lo_half, s2, 0.0), axis=-1, keepdims=True),
                        jnp.sum(jnp.where(lo_half, 0.0, s2), axis=-1, keepdims=True)))
    ml_num2 = [_dot(_bf(ml_s2[pp]), _bf(_blockdiag_rows(ml_v[2 * pp], ml_v[2 * pp + 1]))) for pp in ml_pairs]
    yield

    y_parts = []
    for g in range(SSD_GROUPS):
        gs = slice(g * 512, (g + 1) * 512)
        for pp in range(4):
            ps = slice((g * 4 + pp) * 128, (g * 4 + pp + 1) * 128)
            seg = ac_f[:, ps] - ctr[:, ps]
            m2 = _bf(ssd_cb2[g] * jnp.exp2(jnp.where(causal2, seg, NEG_INF)))
            xp = xt[:, ps]
            xb = _bf(jnp.concatenate([jnp.where(lo_half, xp, 0.0), jnp.where(lo_half, 0.0, xp)], axis=0))
            y_parts.append(_dot(m2, xb) + eac_f[:, ps] * ssd_yi[g][:, pp * 128:(pp + 1) * 128])
        ht_ref[:, gs] = ht_ref[:, gs] * eac_end[:, gs] + _dot(_bf(ssd_bm[g].T), xdec[:, gs])
        yield

    hg_o2 = [_dot(hg_a2[pp], _blockdiag_rows(vv_b[:, hg_sl[2 * pp]], vv_b[:, hg_sl[2 * pp + 1]]))
             for pp in hg_pairs]

    for h in range(ML_HEADS):
        pp, j, ln = h // 2, h % 2, LANE_MF + h
        qh = col(C_MQ + h * 128, 128)
        wi = w_inter[:, ln:ln + 1]
        num = ml_num2[pp][:, j * 128:(j + 1) * 128] + wi * ml_qc[h]
        qn = jnp.sum(qh * mn_ref[h:h + 1, :], axis=-1, keepdims=True)
        den = ml_dens[pp][j] + wi * qn
        hval = num * (1.0 / jnp.maximum(jnp.abs(den), inv_floor[:, ln:ln + 1]))
        hn = _rms(hval, mlg_ref[:, h * 128:(h + 1) * 128])
        put(SSD_WIDTH + h * 128, _bf(col(C_MO + h * 128, 128) * hn))
        kw = ml_k[h] * wk_s[:, ln:ln + 1]
        sc = sc_s[:, ln:ln + 1]
        mc_ref[h] = sc * mc_ref[h] + _dot(_bf(kw.T), _bf(ml_v[h]))
        mn_ref[h:h + 1, :] = sc * mn_ref[h:h + 1, :] + jnp.sum(kw, axis=0, keepdims=True)
        if h % 2 == 1:
            yield
    mm_ref[...] = jnp.where(ml_lane[0:1], m_i[L - 1:L, :], 0.0)

    y = jnp.concatenate(y_parts, axis=1) + dsk_ref[...] * xs
    y_ssd = _rms(y * col(C_Z, SSD_WIDTH), sg_ref[...])
    put(0, _bf(y_ssd))
    yield

    for h in range(HG_HEADS):
        hs = hg_sl[h]
        o = hg_o2[h // 2][:, (h % 2) * 128:(h % 2 + 1) * 128] + hg_oi[h]
        on = _rms(o, hgg_ref[:, hs])
        put(SSD_WIDTH + ML_WIDTH + h * 128, _bf(on * col(C_HG + h * 128, 128)))
        hgt_ref[h] = hgt_ref[h] * s_dec[:, hs] + _dot(_bf(vv[:, hs].T), k_dec[:, hs])


def _lockstep(gens):
    gens = list(gens)
    while gens:
        alive = []
        for g in gens:
            try:
                next(g)
                alive.append(g)
            except StopIteration:
                pass
        gens = alive


def _mixer_kernel(*refs, t_rows, n_chunks, n_seq, layer, depth, has_state, has_prev, ring):
    it = iter(refs)
    proj_ref = next(it)
    state0 = [next(it) for _ in range(6)] if has_state else None
    sp_ref, cw_ref, cb_ref, dsk_ref, sg_ref, mlg_ref, hgg_ref = (next(it) for _ in range(7))
    e3_ref, dsel_ref, causal_ref, ce3_ref, hmask_ref, isq_ref = (next(it) for _ in range(6))
    if has_prev:
        for _ in range(6):
            next(it)
    mix_ref, conv_ref, ssd_ref, mc_ref, mn_ref, mm_ref, hg_ref = (next(it) for _ in range(7))
    cbuf_ref, ht_ref, hgt_ref = (next(it) for _ in range(3))
    padded = t_rows < CHUNK
    pbuf_ref = next(it) if padded else None
    if ring:
        pring_ref, ring_sem = next(it), next(it)
    tb = pl.program_id(1)

    if ring:
        nb = pl.num_programs(1)
        n_steps = pl.num_programs(0) * nb
        step_n = pl.program_id(0) * nb + tb

        def ring_copy(n):
            return pltpu.make_async_copy(
                proj_ref.at[n // nb, pl.ds((n % nb) * t_rows, t_rows), :],
                pring_ref.at[n % RING_SLOTS], ring_sem.at[n % RING_SLOTS])

        @pl.when(step_n == 0)
        def _prime():
            for k in range(RING_SLOTS - 1):
                ring_copy(jnp.int32(k)).start()

        @pl.when(step_n + RING_SLOTS - 1 < n_steps)
        def _ahead():
            ring_copy(step_n + RING_SLOTS - 1).start()

        ring_copy(step_n).wait()
        slot = step_n % RING_SLOTS
    li = 0 if has_prev else layer
    out_refs = (conv_ref, ssd_ref, mc_ref, mn_ref, mm_ref, hg_ref)

    @pl.when(tb == 0)
    def _init():
        if not has_prev:
            for other in range(depth):
                if other != layer:
                    for r in out_refs:
                        r[other] = jnp.zeros(r.shape[1:], jnp.float32)
        for s in range(n_seq):
            if has_state:
                conv0_ref, ssd0_ref, mc0_ref, mn0_ref, mm0_ref, hg0_ref = state0
                cbuf_ref[s, CONV_TAIL:SUBLANES, :] = conv0_ref[0, s]
                ht_ref[s] = ssd0_ref[0, s].reshape(SSD_WIDTH, SSD_STATE).T
                mc_ref[li, s] = mc0_ref[0, s]
                mn_ref[li, s] = mn0_ref[0, s]
                mm_ref[li, s] = mm0_ref[0, s]
                for h in range(HG_HEADS):
                    hgt_ref[s, h] = hg0_ref[0, s, h].T
            else:
                cbuf_ref[s, CONV_TAIL:SUBLANES, :] = jnp.zeros((CONV_W - 1, CONV_CH), jnp.float32)
                ht_ref[s] = jnp.zeros(ht_ref.shape[1:], jnp.float32)
                mc_ref[li, s] = jnp.zeros(mc_ref.shape[2:], jnp.float32)
                mn_ref[li, s] = jnp.zeros(mn_ref.shape[2:], jnp.float32)
                mm_ref[li, s] = jnp.zeros(mm_ref.shape[2:], jnp.float32)
                hgt_ref[s] = jnp.zeros(hgt_ref.shape[1:], jnp.float32)

    def chunk(s, col, put):
        return _mixer_chunk(
            col, put, sp_ref=sp_ref, cw_ref=cw_ref, cb_ref=cb_ref, dsk_ref=dsk_ref, sg_ref=sg_ref, mlg_ref=mlg_ref,
            hgg_ref=hgg_ref, e3_ref=e3_ref, dsel_ref=dsel_ref, causal_ref=causal_ref, ce3_ref=ce3_ref,
            hmask_ref=hmask_ref, isq_ref=isq_ref, mc_ref=mc_ref.at[li, s], mn_ref=mn_ref.at[li, s],
            mm_ref=mm_ref.at[li, s], ht_ref=ht_ref.at[s], hgt_ref=hgt_ref.at[s], cbuf_ref=cbuf_ref.at[s],
            t_valid=t_rows if padded else CHUNK)

    if padded:
        gens = []
        for s in range(n_seq):
            pbuf_ref[s, 0:t_rows, :] = proj_ref[s]
            pbuf_ref[s, t_rows:CHUNK, :] = jnp.zeros((CHUNK - t_rows, PROJ_COLS), jnp.float32)

            def col(c0, width, s=s):
                return pbuf_ref[s, :, c0:c0 + width]

            def put(c0, val, s=s):
                mix_ref[s, :, c0:c0 + val.shape[1]] = val[0:t_rows]

            gens.append(chunk(s, col, put))
        _lockstep(gens)
    else:
        def body(ci, carry):
            rows = pl.ds(pl.multiple_of(ci * CHUNK, CHUNK), CHUNK)
            gens = []
            for s in range(n_seq):
                def col(c0, width, s=s):
                    if ring:
                        return pring_ref[slot, rows, c0:c0 + width]
                    return proj_ref[s, rows, c0:c0 + width]

                def put(c0, val, s=s):
                    mix_ref[s, rows, c0:c0 + val.shape[1]] = val

                gens.append(chunk(s, col, put))
            _lockstep(gens)
            return carry
        lax.fori_loop(0, n_chunks, body, 0, unroll=2 if n_chunks % 2 == 0 else 1)

    @pl.when(tb == pl.num_programs(1) - 1)
    def _fin():
        for s in range(n_seq):
            conv_ref[li, s] = cbuf_ref[s, CONV_TAIL:SUBLANES, :]
            ssd_ref[li, s] = ht_ref[s].T.reshape(SSD_HEADS, SSD_HEADDIM, SSD_STATE)
            for h in range(HG_HEADS):
                hg_ref[li, s, h] = hgt_ref[s, h].T


def _mixer(proj, state_in, prev, params, consts, layer, depth):
    b, t, _ = proj.shape
    if t < CHUNK:
        tblk = t
    else:
        tblk = min(TIME_BLOCK, t)
        assert t % tblk == 0 and tblk % CHUNK == 0
    ns = SEQ_PER_STEP
    assert b % ns == 0
    c2 = lambda i, j: (0, 0)
    c3 = lambda i, j: (0, 0, 0)
    state_dims = [(CONV_W - 1, CONV_CH), (SSD_HEADS, SSD_HEADDIM, SSD_STATE), (ML_HEADS, ML_HD, ML_HD),
                  (ML_HEADS, ML_HD), (1, 128), (HG_HEADS, HG_HD, HG_HD)]

    def state_spec(dims, layers, first):
        return pl.BlockSpec((layers, ns) + dims, lambda i, j: (first, i) + (0,) * len(dims))

    in_state_specs = [state_spec(d, 1, layer) for d in state_dims]
    out_state_specs = [state_spec(d, 1, layer) if prev is not None else state_spec(d, depth, 0) for d in state_dims]
    state_shapes = [
        jax.ShapeDtypeStruct((depth, b, CONV_W - 1, CONV_CH), jnp.float32),
        jax.ShapeDtypeStruct((depth, b, SSD_HEADS, SSD_HEADDIM, SSD_STATE), jnp.float32),
        jax.ShapeDtypeStruct((depth, b, ML_HEADS, ML_HD, ML_HD), jnp.float32),
        jax.ShapeDtypeStruct((depth, b, ML_HEADS, ML_HD), jnp.float32),
        jax.ShapeDtypeStruct((depth, b, 1, 128), jnp.float32),
        jax.ShapeDtypeStruct((depth, b, HG_HEADS, HG_HD, HG_HD), jnp.float32),
    ]
    ring = t >= CHUNK and ns == 1 and b * (t // tblk) >= RING_SLOTS - 1
    args = [proj]
    in_specs = [pl.BlockSpec(memory_space=pl.ANY) if ring
                else pl.BlockSpec((ns, tblk, PROJ_COLS), lambda i, j: (i, j, 0))]
    if state_in is not None:
        args += list(state_in)
        in_specs += in_state_specs
    args += list(params) + list(consts)
    in_specs += [pl.BlockSpec(a.shape, c3 if a.ndim == 3 else c2) for a in list(params) + list(consts)]
    aliases = {}
    if prev is not None:
        for k, a in enumerate(prev):
            aliases[len(args)] = 1 + k
            args.append(a)
            in_specs.append(pl.BlockSpec(memory_space=pl.ANY))
    scratch = [pltpu.VMEM((ns, SUBLANES + CHUNK, CONV_CH), jnp.float32),
               pltpu.VMEM((ns, SSD_STATE, SSD_WIDTH), jnp.float32),
               pltpu.VMEM((ns, HG_HEADS, HG_HD, HG_HD), jnp.float32)]
    if t < CHUNK:
        scratch.append(pltpu.VMEM((ns, CHUNK, PROJ_COLS), jnp.float32))
    if ring:
        scratch += [pltpu.VMEM((RING_SLOTS, tblk, PROJ_COLS), jnp.float32), pltpu.SemaphoreType.DMA((RING_SLOTS,))]
    return pl.pallas_call(
        functools.partial(_mixer_kernel, t_rows=tblk, n_chunks=max(tblk // CHUNK, 1), n_seq=ns, layer=layer,
                          depth=depth, has_state=state_in is not None, has_prev=prev is not None, ring=ring),
        grid=(b // ns, t // tblk),
        in_specs=in_specs,
        out_specs=[pl.BlockSpec((ns, tblk, D_MIX), lambda i, j: (i, j, 0))] + out_state_specs,
        out_shape=[jax.ShapeDtypeStruct((b, t, D_MIX), jnp.bfloat16)] + state_shapes,
        scratch_shapes=scratch,
        input_output_aliases=aliases,
        compiler_params=pltpu.CompilerParams(
            dimension_semantics=("arbitrary" if ring else "parallel", "arbitrary"), vmem_limit_bytes=VMEM_LIMIT),
        name="mixer",
    )(*args)


_W_IN_GROUPS = (
    (1024, C_Z), (1536, C_XBC), (16, C_SM + LANE_DT), (512, C_MQ), (512, C_MK), (512, C_MV),
    (4, C_SM + LANE_MI), (4, C_SM + LANE_MF), (512, C_MO), (512, C_HQ), (512, C_HK), (512, C_HI), (512, C_HG),
)
W_IN_COLS = sum(width for width, _ in _W_IN_GROUPS)
REGROUP_ROWS = 128


def _regroup_kernel(w_ref, o_ref):
    o_ref[:, C_SM:C_SM + 128] = jnp.zeros((REGROUP_ROWS, 128), jnp.bfloat16)
    src = 0
    for width, dst in _W_IN_GROUPS:
        o_ref[:, dst:dst + width] = w_ref[:, src:src + width].astype(jnp.bfloat16)
        src += width


def _regroup_w_in(w):
    depth = w.shape[0]
    return pl.pallas_call(
        _regroup_kernel,
        grid=(depth, D_MODEL // REGROUP_ROWS),
        in_specs=[pl.BlockSpec((None, REGROUP_ROWS, W_IN_COLS), lambda l, i: (l, i, 0))],
        out_specs=pl.BlockSpec((None, REGROUP_ROWS, W_COLS), lambda l, i: (l, i, 0)),
        out_shape=jax.ShapeDtypeStruct((depth, D_MODEL, W_COLS), jnp.bfloat16),
        compiler_params=pltpu.CompilerParams(
            dimension_semantics=("parallel", "parallel"), vmem_limit_bytes=VMEM_LIMIT),
        name="regroup_w_in",
    )(w)


def _small_params(dt_bias, a_log, ml_bi, ml_bf):
    sp = jnp.zeros((8, 128), jnp.float32)
    sp = sp.at[0, LANE_DT:LANE_DT + SSD_HEADS].set(dt_bias)
    sp = sp.at[1, LANE_DT:LANE_DT + SSD_HEADS].set(a_log)
    sp = sp.at[2, LANE_MF:LANE_MF + ML_HEADS].set(ml_bi)
    sp = sp.at[3, LANE_MF:LANE_MF + ML_HEADS].set(ml_bf)
    return sp


def _run_trunk(x, state_in, lp, norm_final, consts, tm_proj, tm_ffn):
    b, t, _ = x.shape
    assert t % CHUNK == 0 or t < CHUNK, "a partial chunk is only supported for single-chunk sequences"
    depth = len(lp)
    x2d = x.reshape(b * t, D_MODEL)
    states = None
    for l in range(depth):
        p = lp[l]
        params = (p["sp"], p["conv_w"], p["conv_b"], p["d_skip"], p["ssd_gain"], p["ml_gain"], p["hg_gain"])
        proj = _in_proj(x2d, p["norm_mix"], p["w_in"], p["hg_lb"], tm_proj, l)
        res = _mixer(proj.reshape(b, t, PROJ_COLS), state_in, states, params, consts, l, depth)
        states = res[1:]
        x2d = _out_ffn(x2d, res[0].reshape(b * t, D_MIX), p["w_out"], p["norm_ffn"], p["w_ffn_in"],
                       p["w_ffn_out"], norm_final, tm_ffn, l, l == depth - 1)
    conv, ssd, mc, mn, mm_p, hg = states
    return x2d.reshape(b, t, D_MODEL), (conv, ssd, mc, mn, mm_p[:, :, 0, LANE_MF:LANE_MF + ML_HEADS], hg)


def kernel(x_prompt, x_sample, state_conv, state_ssd, state_mlstm_c, state_mlstm_n, state_mlstm_m, state_hgrn,
           norm_mix, w_in, conv_w, conv_b, dt_bias, a_log, d_skip, ssd_gain, ml_bi, ml_bf, ml_gain,
           hg_lb, hg_gain, w_out, norm_ffn, w_ffn_in, w_ffn_out, norm_final):
    depth = w_in.shape[0]
    f32 = jnp.float32
    consts = _mixer_constants()
    bf16 = jnp.bfloat16
    w_in_b, w_out_b, w_fi_b, w_fo_b = (_regroup_w_in(w_in), w_out.astype(bf16), w_ffn_in.astype(bf16),
                                       w_ffn_out.astype(bf16))
    lp = []
    for l in range(depth):
        lp.append({
            "norm_mix": norm_mix[l].reshape(1, D_MODEL),
            "w_in": w_in_b,
            "sp": _small_params(dt_bias[l], a_log[l], ml_bi[l], ml_bf[l]),
            "conv_w": conv_w[l],
            "conv_b": conv_b[l].reshape(1, CONV_CH),
            "d_skip": jnp.repeat(d_skip[l], SSD_HEADDIM).reshape(1, SSD_WIDTH),
            "ssd_gain": ssd_gain[l].reshape(1, SSD_WIDTH),
            "ml_gain": ml_gain[l].reshape(1, ML_WIDTH),
            "hg_lb": hg_lb.astype(f32),
            "hg_gain": hg_gain[l].reshape(1, HG_WIDTH),
            "w_out": w_out_b,
            "norm_ffn": norm_ffn[l].reshape(1, D_MODEL),
            "w_ffn_in": w_fi_b,
            "w_ffn_out": w_fo_b,
        })
    nf = norm_final.reshape(1, D_MODEL)

    y_prompt, p_outs = _run_trunk(x_prompt, None, lp, nf, consts, 256, 512)
    bs = x_sample.shape[0]
    mm_p = jnp.pad(state_mlstm_m.astype(f32)[:, :, None, :],
                   ((0, 0), (0, 0), (0, 0), (LANE_MF, 128 - LANE_MF - ML_HEADS)))
    s_in = (state_conv, state_ssd, state_mlstm_c, state_mlstm_n, mm_p, state_hgrn)
    y_sample, s_outs = _run_trunk(x_sample, s_in, lp, nf, consts, 256, 512)
    return (y_prompt, y_sample) + p_outs + s_outs
```
